```python
import math
import jax, jax.numpy as jnp
from jax import lax
import numpy as np

D_MODEL = 1024
BATCH = 4
SEQ = 8192
DEPTH = 2

CHUNK = 64
Q_BLOCK = 128
PLE_DIM = 256
RMS_EPS = 1e-6

SB_HEADS = 8
SB_HEAD_DIM = 64
SB_WIDTH = SB_HEADS * SB_HEAD_DIM

DA_HEADS = 4
DA_HEAD_DIM = 64
DA_V_DIM = 2 * DA_HEAD_DIM
DA_QK_WIDTH = DA_HEADS * 2 * DA_HEAD_DIM
DA_WIDTH = DA_HEADS * DA_V_DIM

ROPE_THETA = 500000.0
ROPE_DIM = DA_HEAD_DIM // 4

N_GROUPS = 4
EXPERTS_PER_GROUP = 8
N_EXPERTS = N_GROUPS * EXPERTS_PER_GROUP
TOP_K = 2
EXPERT_HIDDEN = 256
MOE_BLOCK = 256

IN_SIZES = (SB_WIDTH, SB_WIDTH, SB_WIDTH, DA_QK_WIDTH, DA_QK_WIDTH, DA_WIDTH, D_MODEL, D_MODEL)
IN_WIDTH = sum(IN_SIZES)
IN_OFFSETS = tuple(int(o) for o in np.cumsum(IN_SIZES)[:-1])

kernel_name = "hybrid_stickbreak_diffattn_hmoe_block"


def rmsnorm(x, g):
    xf = x.astype(jnp.float32)
    y = xf * lax.rsqrt(jnp.mean(xf * xf, axis=-1, keepdims=True) + RMS_EPS)
    return (y * g.astype(jnp.float32)).astype(x.dtype)


def apply_partial_rope(t, cos, sin):
    half = ROPE_DIM // 2
    r1 = t[..., :half]
    r2 = t[..., half:ROPE_DIM]
    rot = jnp.concatenate([r1 * cos - r2 * sin, r2 * cos + r1 * sin], axis=-1)
    return jnp.concatenate([rot, t[..., ROPE_DIM:]], axis=-1)


def stick_breaking_attention(q, k, v):
    B, H, S, d = q.shape
    scale = d ** -0.5
    key_idx = jnp.arange(S)

    def block(j):
        start = j * Q_BLOCK
        qb = lax.dynamic_slice_in_dim(q, start, Q_BLOCK, axis=2)
        z = jnp.einsum('bhqd,bhkd->bhqk', qb, k, preferred_element_type=jnp.float32) * scale
        q_idx = start + jnp.arange(Q_BLOCK)
        strict = key_idx[None, :] < q_idx[:, None]
        log_stay = jnp.where(strict, jax.nn.log_sigmoid(-z), 0.0)
        between = lax.cumsum(log_stay, axis=3, reverse=True) - log_stay
        w = jnp.where(strict, jnp.exp(jax.nn.log_sigmoid(z) + between), 0.0)
        return jnp.einsum('bhqk,bhkd->bhqd', w.astype(v.dtype), v)

    out = lax.map(block, jnp.arange(S // Q_BLOCK))
    return out.transpose(1, 0, 3, 2, 4).reshape(B, S, H, d)


def differential_attention(q, k, v, lam):
    B, H, _, S, d = q.shape
    dv = v.shape[-1]
    scale = d ** -0.5
    key_chunk = jnp.arange(S) // CHUNK
    neg = jnp.finfo(jnp.float32).min

    def block(j):
        start = j * Q_BLOCK
        qb = lax.dynamic_slice_in_dim(q, start, Q_BLOCK, axis=3)
        s = jnp.einsum('bhmqd,bhmkd->bhmqk', qb, k, preferred_element_type=jnp.float32) * scale
        q_chunk = (start + jnp.arange(Q_BLOCK)) // CHUNK
        allowed = key_chunk[None, :] <= q_chunk[:, None]
        a = jax.nn.softmax(jnp.where(allowed, s, neg), axis=-1)
        a = a[:, :, 0] - lam * a[:, :, 1]
        return jnp.einsum('bhqk,bhkd->bhqd', a.astype(v.dtype), v)

    out = lax.map(block, jnp.arange(S // Q_BLOCK))
    return out.transpose(1, 0, 3, 2, 4).reshape(B, S, H, dv)


def hierarchical_moe(h, w_rg, b_rg, w_re, b_re, w_gate, w_up, w_down):
    B, S, D = h.shape
    N = B * S
    t = h.reshape(N, D)
    rows = jnp.arange(N)
    g_logits = (t @ w_rg).astype(jnp.float32) + b_rg.astype(jnp.float32)
    g_prob = jax.nn.softmax(g_logits, axis=-1)
    g_sel = jnp.argmax(g_logits, axis=-1)
    g_p = g_prob[rows, g_sel]
    e_logits = ((t @ w_re).astype(jnp.float32) + b_re.astype(jnp.float32)).reshape(N, N_GROUPS, EXPERTS_PER_GROUP)
    e_prob = jax.nn.softmax(e_logits[rows, g_sel], axis=-1)
    top_p, top_i = lax.top_k(e_prob, TOP_K)
    gates = g_p[:, None] * top_p / jnp.sum(top_p, axis=-1, keepdims=True)

    expert_id = (g_sel[:, None] * EXPERTS_PER_GROUP + top_i).reshape(-1)
    token_id = jnp.repeat(rows, TOP_K)
    gate_flat = gates.reshape(-1)
    A = N * TOP_K

    order = jnp.argsort(expert_id)
    se, st, sg = expert_id[order], token_id[order], gate_flat[order]
    counts = jnp.zeros((N_EXPERTS,), jnp.int32).at[expert_id].add(1)
    pcounts = (counts + MOE_BLOCK - 1) // MOE_BLOCK * MOE_BLOCK
    starts = jnp.cumsum(counts) - counts
    pends = jnp.cumsum(pcounts)
    pstarts = pends - pcounts
    dest = pstarts[se] + jnp.arange(A) - starts[se]
    P = ((A + MOE_BLOCK - 1) // MOE_BLOCK) * MOE_BLOCK + N_EXPERTS * MOE_BLOCK
    n_blk = P // MOE_BLOCK
    slot_tok = jnp.full((P,), N, jnp.int32).at[dest].set(st)
    slot_gate = jnp.zeros((P,), jnp.float32).at[dest].set(sg)
    blk_e = jnp.minimum(jnp.searchsorted(pends, jnp.arange(n_blk) * MOE_BLOCK, side='right'), N_EXPERTS - 1)
    xs = jnp.concatenate([t, jnp.zeros((1, D), t.dtype)], axis=0)[slot_tok].reshape(n_blk, MOE_BLOCK, D)

    def expert_block(args):
        xb, e = args
        hid = jax.nn.silu(xb @ w_gate[e]) * (xb @ w_up[e])
        return hid @ w_down[e]

    ys = lax.map(expert_block, (xs, blk_e)).reshape(P, D)
    ys = ys * slot_gate[:, None].astype(ys.dtype)
    out = jax.ops.segment_sum(ys, slot_tok, num_segments=N + 1)[:N]
    return out.reshape(B, S, D)


def setup_inputs(seed: int = 0) -> dict:
    key = jax.random.key(seed)
    ks = jax.random.split(key, 26)
    D = D_MODEL

    def nrm(k, shape, scale):
        return jax.random.normal(k, shape, jnp.float32) * scale

    def gain(k, shape):
        return 1.0 + 0.02 * jax.random.normal(k, shape, jnp.float32)

    offset = jax.random.randint(ks[2], (BATCH, 1), 0, 4096, dtype=jnp.int32)
    positions = (offset + jnp.arange(SEQ, dtype=jnp.int32)[None, :]).astype(jnp.int32)
    return {
        "x": nrm(ks[0], (BATCH, SEQ, D), 1.0),
        "p": nrm(ks[1], (DEPTH, BATCH, SEQ, PLE_DIM), 1.0),
        "positions": positions,
        "g_mix": gain(ks[3], (DEPTH, D)),
        "w_in": nrm(ks[4], (DEPTH, D, IN_WIDTH), D ** -0.5),
        "lam_q1": nrm(ks[5], (DEPTH, DA_HEAD_DIM), 0.1),
        "lam_k1": nrm(ks[6], (DEPTH, DA_HEAD_DIM), 0.1),
        "lam_q2": nrm(ks[7], (DEPTH, DA_HEAD_DIM), 0.1),
        "lam_k2": nrm(ks[8], (DEPTH, DA_HEAD_DIM), 0.1),
        "g_subln": gain(ks[9], (DEPTH, DA_V_DIM)),
        "w_br_a": nrm(ks[10], (DEPTH, SB_WIDTH, D), SB_WIDTH ** -0.5),
        "w_br_b": nrm(ks[11], (DEPTH, DA_WIDTH, D), DA_WIDTH ** -0.5),
        "w_o": nrm(ks[12], (DEPTH, D, D), D ** -0.5),
        "g_ffn": gain(ks[13], (DEPTH, D)),
        "w_router_group": nrm(ks[14], (DEPTH, D, N_GROUPS), D ** -0.5),
        "b_router_group": nrm(ks[15], (DEPTH, N_GROUPS), 0.01),
        "w_router_expert": nrm(ks[16], (DEPTH, D, N_EXPERTS), D ** -0.5),
        "b_router_expert": nrm(ks[17], (DEPTH, N_EXPERTS), 0.01),
        "w_exp_gate": nrm(ks[18], (DEPTH, N_EXPERTS, D, EXPERT_HIDDEN), D ** -0.5),
        "w_exp_up": nrm(ks[19], (DEPTH, N_EXPERTS, D, EXPERT_HIDDEN), D ** -0.5),
        "w_exp_down": nrm(ks[20], (DEPTH, N_EXPERTS, EXPERT_HIDDEN, D), EXPERT_HIDDEN ** -0.5),
        "g_ple": gain(ks[21], (DEPTH, D)),
        "w_ple": nrm(ks[22], (DEPTH, PLE_DIM, D), PLE_DIM ** -0.5),
        "w_ple_gate": nrm(ks[23], (DEPTH, D, D), D ** -0.5),
        "g_final": gain(ks[24], (D,)),
    }


def reference(x, p, positions, g_mix, w_in, lam_q1, lam_k1, lam_q2, lam_k2, g_subln,
              w_br_a, w_br_b, w_o, g_ffn, w_router_group, b_router_group,
              w_router_expert, b_router_expert, w_exp_gate, w_exp_up, w_exp_down,
              g_ple, w_ple, w_ple_gate, g_final):
    B, S, D = x.shape
    inv_freq = ROPE_THETA ** (-jnp.arange(0, ROPE_DIM, 2, dtype=jnp.float32) / ROPE_DIM)
    ang = positions.astype(jnp.float32)[..., None] * inv_freq
    cos = jnp.cos(ang)[:, :, None, None, :].astype(x.dtype)
    sin = jnp.sin(ang)[:, :, None, None, :].astype(x.dtype)

    for i in range(DEPTH):
        lam_init = 0.8 - 0.6 * math.exp(-0.3 * i)
        h = rmsnorm(x, g_mix[i])
        u = h @ w_in[i]
        sb_q, sb_k, sb_v, da_q, da_k, da_v, gate_a, gate_b = jnp.split(u, IN_OFFSETS, axis=-1)

        to_heads = lambda t: t.reshape(B, S, SB_HEADS, SB_HEAD_DIM).transpose(0, 2, 1, 3)
        y_a = stick_breaking_attention(to_heads(sb_q), to_heads(sb_k), to_heads(sb_v))
        y_a = y_a.reshape(B, S, SB_WIDTH) @ w_br_a[i]

        dq = apply_partial_rope(da_q.reshape(B, S, DA_HEADS, 2, DA_HEAD_DIM), cos, sin)
        dk = apply_partial_rope(da_k.reshape(B, S, DA_HEADS, 2, DA_HEAD_DIM), cos, sin)
        dq = dq.transpose(0, 2, 3, 1, 4)
        dk = dk.transpose(0, 2, 3, 1, 4)
        dv = da_v.reshape(B, S, DA_HEADS, DA_V_DIM).transpose(0, 2, 1, 3)
        lam = (jnp.exp(jnp.sum(lam_q1[i].astype(jnp.float32) * lam_k1[i].astype(jnp.float32)))
               - jnp.exp(jnp.sum(lam_q2[i].astype(jnp.float32) * lam_k2[i].astype(jnp.float32)))
               + lam_init)
        y_b = differential_attention(dq, dk, dv, lam)
        y_b = rmsnorm(y_b, g_subln[i]) * (1.0 - lam_init)
        y_b = y_b.reshape(B, S, DA_WIDTH) @ w_br_b[i]

        merged = jax.nn.sigmoid(gate_a) * y_a + jax.nn.sigmoid(gate_b) * y_b
        x = x + merged @ w_o[i]

        x = x + hierarchical_moe(rmsnorm(x, g_ffn[i]), w_router_group[i], b_router_group[i],
                                 w_router_expert[i], b_router_expert[i],
                                 w_exp_gate[i], w_exp_up[i], w_exp_down[i])

        ple_gate = jax.nn.sigmoid(rmsnorm(x, g_ple[i]) @ w_ple_gate[i])
        x = x + (p[i] @ w_ple[i]) * ple_gate

    return rmsnorm(x, g_final)
```

```python
import functools
import math

import numpy as np
import jax
import jax.numpy as jnp
from jax import lax
from jax.experimental import pallas as pl
from jax.experimental.pallas import tpu as pltpu

F32 = jnp.float32
BF16 = jnp.bfloat16
I32 = jnp.int32

D_MODEL = 1024
PLE_DIM = 256
RMS_EPS = 1e-6
CHUNK = 64
HEAD_DIM = 64
LANES = 128
SB_WIDTH = 512
DA_WIDTH = 512
ROPE_THETA = 500000.0
ROPE_DIM = 16
N_GROUPS = 4
EXPERTS_PER_GROUP = 8
N_EXPERTS = 32
EXPERT_HIDDEN = 256
MOE_BLOCK = 256
IN_WIDTH = 5120
OFF_SBQ, OFF_SBK, OFF_SBV, OFF_DAQ, OFF_DAK, OFF_DAV, OFF_GA, OFF_GB = (
    0, 512, 1024, 1536, 2048, 2560, 3072, 4096)
QK_SCALE = HEAD_DIM ** -0.5
NEG_BIG = -1e30
GROUP_LANE0 = N_EXPERTS
VMEM_LIMIT = 56 * 1024 * 1024


def _cparams(sem):
    return pltpu.CompilerParams(dimension_semantics=sem, vmem_limit_bytes=VMEM_LIMIT)


def _rms(x, g):
    return x * lax.rsqrt(jnp.mean(x * x, axis=-1, keepdims=True) + RMS_EPS) * g


def _inproj_kernel(x_ref, g_ref, w_ref, c_ref, s1_ref, s2_ref,
                   sbq_ref, sbk_ref, sbv_ref, daq_ref, dak_ref, dav_ref, ga_ref, gb_ref):
    h = _rms(x_ref[...], g_ref[...]).astype(BF16)

    def proj(off, width):
        return jnp.dot(h, w_ref[:, off:off + width], preferred_element_type=F32)

    sbq_ref[...] = (proj(OFF_SBQ, SB_WIDTH) * QK_SCALE).astype(BF16)
    sbk_ref[...] = proj(OFF_SBK, SB_WIDTH).astype(BF16)
    sbv_ref[...] = proj(OFF_SBV, SB_WIDTH).astype(BF16)

    c = c_ref[...]
    s1 = s1_ref[...]
    s2 = s2_ref[...]

    def rope_store(t, out_ref, scale):
        for j in range(t.shape[1] // LANES):
            tc = t[:, j * LANES:(j + 1) * LANES]
            r = tc * c + pltpu.roll(tc, LANES - 8, 1) * s1 + pltpu.roll(tc, 8, 1) * s2
            out_ref[:, j * LANES:(j + 1) * LANES] = (r * scale).astype(BF16)

    rope_store(proj(OFF_DAQ, DA_WIDTH), daq_ref, QK_SCALE)
    rope_store(proj(OFF_DAK, DA_WIDTH), dak_ref, 1.0)
    dav_ref[...] = proj(OFF_DAV, DA_WIDTH).astype(BF16)
    ga_ref[...] = jax.nn.sigmoid(proj(OFF_GA, D_MODEL)).astype(BF16)
    gb_ref[...] = jax.nn.sigmoid(proj(OFF_GB, D_MODEL)).astype(BF16)


def _inproj(x2, g, w_bf, rope_c, rope_s1, rope_s2, tm):
    n = x2.shape[0]
    row = lambda w: pl.BlockSpec((tm, w), lambda i: (i, 0))
    full = lambda a: pl.BlockSpec(a.shape, lambda i: (0,) * a.ndim)
    outs = [jax.ShapeDtypeStruct((n, w), BF16)
            for w in (SB_WIDTH,) * 3 + (DA_WIDTH,) * 3 + (D_MODEL,) * 2]
    return pl.pallas_call(
        _inproj_kernel,
        grid=(n // tm,),
        in_specs=[row(D_MODEL), full(g), full(w_bf), row(LANES), row(LANES), row(LANES)],
        out_specs=[row(s.shape[1]) for s in outs],
        out_shape=outs,
        compiler_params=_cparams(("parallel",)),
        name="inproj",
    )(x2, g, w_bf, rope_c, rope_s1, rope_s2)


def _sb_kernel(q_ref, k_ref, v_ref, u_ref, o_ref, acc_ref, car_ref, *, tq):
    qi = pl.program_id(2)
    q = q_ref[0]
    lane = lax.broadcasted_iota(I32, (1, LANES), 1)
    first = lane < HEAD_DIM
    zero = jnp.zeros_like(q)
    qs = (jnp.where(first, q, zero), jnp.where(first, zero, q))
    u = u_ref[...]
    acc_ref[...] = jnp.zeros_like(acc_ref)
    car_ref[...] = jnp.zeros_like(car_ref)

    def block(kj, diag):
        start = pl.multiple_of(kj * tq, tq)
        k = k_ref[0, pl.ds(start, tq), :]
        v = v_ref[0, pl.ds(start, tq), :]
        if diag:
            row = lax.broadcasted_iota(I32, (tq, tq), 0)
            col = lax.broadcasted_iota(I32, (tq, tq), 1)
            strict = col < row
        for h in range(2):
            z = lax.dot_general(qs[h], k, (((1,), (1,)), ((), ())),
                                preferred_element_type=F32)
            sp = jnp.log(1.0 + jnp.exp(-jnp.abs(z)))
            ls = jnp.minimum(z, 0.0) - sp
            lstay = jnp.minimum(-z, 0.0) - sp
            if diag:
                lstay = jnp.where(strict, lstay, 0.0)
            hi = lstay.astype(BF16)
            lo = (lstay - hi.astype(F32)).astype(BF16)
            later = (jnp.dot(hi, u, preferred_element_type=F32)
                     + jnp.dot(lo, u, preferred_element_type=F32))
            total = later[:, 0:1] + lstay[:, 0:1]
            w = jnp.exp(ls + later + car_ref[h])
            if diag:
                w = jnp.where(strict, w, 0.0)
            acc_ref[h] += jnp.dot(w.astype(BF16), v, preferred_element_type=F32)
            car_ref[h] += total

    block(qi, True)

    def body(i, carry):
        block(qi - 1 - i, False)
        return carry

    lax.fori_loop(0, qi, body, 0)
    o_ref[0] = jnp.where(first, acc_ref[0], acc_ref[1]).astype(o_ref.dtype)


def _sb_attention(q, k, v, tq):
    b, s, width = q.shape
    later = jnp.asarray(np.tril(np.ones((tq, tq), np.float32), -1), BF16)
    return pl.pallas_call(
        functools.partial(_sb_kernel, tq=tq),
        grid=(b, width // LANES, s // tq),
        in_specs=[
            pl.BlockSpec((1, tq, LANES), lambda bi, p, qi: (bi, qi, p)),
            pl.BlockSpec((1, s, LANES), lambda bi, p, qi: (bi, 0, p)),
            pl.BlockSpec((1, s, LANES), lambda bi, p, qi: (bi, 0, p)),
            pl.BlockSpec((tq, tq), lambda bi, p, qi: (0, 0)),
        ],
        out_specs=pl.BlockSpec((1, tq, LANES), lambda bi, p, qi: (bi, qi, p)),
        out_shape=jax.ShapeDtypeStruct((b, s, width), BF16),
        scratch_shapes=[pltpu.VMEM((2, tq, LANES), F32), pltpu.VMEM((2, tq, 1), F32)],
        compiler_params=_cparams(("parallel", "parallel", "arbitrary")),
        name="sb_attn",
    )(q, k, v, later)


def _da_kernel(q_ref, k_ref, v_ref, lq1_ref, lk1_ref, lq2_ref, lk2_ref, g_ref, o_ref,
               acc_ref, m_ref, l_ref, *, tq, lam_init):
    qi = pl.program_id(2)
    q = q_ref[0]
    lane = lax.broadcasted_iota(I32, (1, LANES), 1)
    first = lane < HEAD_DIM
    zero = jnp.zeros_like(q)
    qs = (jnp.where(first, q, zero), jnp.where(first, zero, q))
    acc_ref[...] = jnp.zeros_like(acc_ref)
    l_ref[...] = jnp.zeros_like(l_ref)
    m_ref[...] = jnp.full_like(m_ref, NEG_BIG)

    def block(kj, diag):
        start = pl.multiple_of(kj * tq, tq)
        k = k_ref[0, pl.ds(start, tq), :]
        v = v_ref[0, pl.ds(start, tq), :]
        if diag:
            row = lax.broadcasted_iota(I32, (tq, tq), 0)
            col = lax.broadcasted_iota(I32, (tq, tq), 1)
            allowed = (col // CHUNK) <= (row // CHUNK)
        for m in range(2):
            s = lax.dot_general(qs[m], k, (((1,), (1,)), ((), ())),
                                preferred_element_type=F32)
            if diag:
                s = jnp.where(allowed, s, NEG_BIG)
            m_prev = m_ref[m]
            m_new = jnp.maximum(m_prev, jnp.max(s, axis=-1, keepdims=True))
            p = jnp.exp(s - m_new)
            alpha = jnp.exp(m_prev - m_new)
            l_ref[m] = alpha * l_ref[m] + jnp.sum(p, axis=-1, keepdims=True)
            acc_ref[m] = alpha * acc_ref[m] + jnp.dot(p.astype(BF16), v,
                                                      preferred_element_type=F32)
            m_ref[m] = m_new

    block(qi, True)

    def body(i, carry):
        block(qi - 1 - i, False)
        return carry

    lax.fori_loop(0, qi, body, 0)

    lam = (jnp.exp(jnp.sum(lq1_ref[...] * lk1_ref[...], axis=-1, keepdims=True))
           - jnp.exp(jnp.sum(lq2_ref[...] * lk2_ref[...], axis=-1, keepdims=True))
           + lam_init)
    o = acc_ref[0] / l_ref[0] - lam * (acc_ref[1] / l_ref[1])
    o_ref[0] = (_rms(o, g_ref[...]) * (1.0 - lam_init)).astype(o_ref.dtype)


def _da_attention(q, k, v, lq1, lk1, lq2, lk2, g_subln, tq, lam_init):
    b, s, width = q.shape
    vec = lambda a: pl.BlockSpec(a.shape, lambda bi, h, qi: (0, 0))
    return pl.pallas_call(
        functools.partial(_da_kernel, tq=tq, lam_init=lam_init),
        grid=(b, width // LANES, s // tq),
        in_specs=[
            pl.BlockSpec((1, tq, LANES), lambda bi, h, qi: (bi, qi, h)),
            pl.BlockSpec((1, s, LANES), lambda bi, h, qi: (bi, 0, h)),
            pl.BlockSpec((1, s, LANES), lambda bi, h, qi: (bi, 0, h)),
            vec(lq1), vec(lk1), vec(lq2), vec(lk2), vec(g_subln),
        ],
        out_specs=pl.BlockSpec((1, tq, LANES), lambda bi, h, qi: (bi, qi, h)),
        out_shape=jax.ShapeDtypeStruct((b, s, width), BF16),
        scratch_shapes=[pltpu.VMEM((2, tq, LANES), F32), pltpu.VMEM((2, tq, 1), F32),
                        pltpu.VMEM((2, tq, 1), F32)],
        compiler_params=_cparams(("parallel", "parallel", "arbitrary")),
        name="da_attn",
    )(q, k, v, lq1, lk1, lq2, lk2, g_subln)


def _merge_kernel(ya_ref, yb_ref, ga_ref, gb_ref, x_ref, wa_ref, wb_ref, wo_ref, g_ref,
                  wrh_ref, wrl_ref, br_ref, tri_ref,
                  x1_ref, h2_ref, route_ref, gate_ref, cnt_ref, run_ref):
    @pl.when(pl.program_id(0) == 0)
    def _():
        run_ref[...] = jnp.zeros_like(run_ref)

    pa = jnp.dot(ya_ref[...], wa_ref[...], preferred_element_type=F32)
    pb = jnp.dot(yb_ref[...], wb_ref[...], preferred_element_type=F32)
    merged = ga_ref[...].astype(F32) * pa + gb_ref[...].astype(F32) * pb
    x1 = x_ref[...] + jnp.dot(merged.astype(BF16), wo_ref[...], preferred_element_type=F32)
    x1_ref[...] = x1
    h2 = _rms(x1, g_ref[...])
    h2_ref[...] = h2

    hh = h2.astype(BF16)
    hl = (h2 - hh.astype(F32)).astype(BF16)
    logits = (jnp.dot(hh, wrh_ref[...], preferred_element_type=F32)
              + jnp.dot(hh, wrl_ref[...], preferred_element_type=F32)
              + jnp.dot(hl, wrh_ref[...], preferred_element_type=F32)
              + br_ref[...])

    tm = logits.shape[0]
    lane = lax.broadcasted_iota(I32, (tm, LANES), 1)
    far = jnp.int32(4 * LANES)
    is_group = (lane >= GROUP_LANE0) & (lane < GROUP_LANE0 + N_GROUPS)
    gl = jnp.where(is_group, logits, -jnp.inf)
    gmax = jnp.max(gl, axis=-1, keepdims=True)
    gsel = jnp.min(jnp.where(gl == gmax, lane, far), axis=-1, keepdims=True) - GROUP_LANE0
    gsum = jnp.sum(jnp.where(is_group, jnp.exp(gl - gmax), 0.0), axis=-1, keepdims=True)
    g_p = 1.0 / gsum

    in_group = (lane >> 3) == gsel
    el = jnp.where(in_group, logits, -jnp.inf)
    m1 = jnp.max(el, axis=-1, keepdims=True)
    i1 = jnp.min(jnp.where(el == m1, lane, far), axis=-1, keepdims=True)
    el2 = jnp.where(lane == i1, -jnp.inf, el)
    m2 = jnp.max(el2, axis=-1, keepdims=True)
    i2 = jnp.min(jnp.where(el2 == m2, lane, far), axis=-1, keepdims=True)
    e2 = jnp.exp(m2 - m1)
    gate1 = g_p / (1.0 + e2)
    gate2 = gate1 * e2

    sel1 = lane == i1
    sel2 = lane == i2
    onehot = jnp.where(sel1 | sel2, 1.0, 0.0)
    before = jnp.dot(tri_ref[...], onehot.astype(BF16), preferred_element_type=F32)
    before = before + run_ref[...]
    r1 = jnp.sum(jnp.where(sel1, before, 0.0), axis=-1, keepdims=True).astype(I32)
    r2 = jnp.sum(jnp.where(sel2, before, 0.0), axis=-1, keepdims=True).astype(I32)
    run_ref[...] += jnp.sum(onehot, axis=0, keepdims=True)
    cnt_ref[...] = run_ref[...].astype(I32)

    code1 = (i1 << 16) | r1
    code2 = (i2 << 16) | r2
    route_ref[...] = jnp.where(lane == 0, code1, jnp.where(lane == 1, code2, 0))
    gate_ref[...] = jnp.where(lane == 0, gate1, jnp.where(lane == 1, gate2, 0.0))


def _merge(ya, yb, ga, gb, x2, wa, wb, wo, g_ffn, wr_hi, wr_lo, b_r, tm):
    n = x2.shape[0]
    tri = jnp.asarray(np.tril(np.ones((tm, tm), np.float32), -1), BF16)
    row = lambda w: pl.BlockSpec((tm, w), lambda i: (i, 0))
    full = lambda a: pl.BlockSpec(a.shape, lambda i: (0,) * a.ndim)
    return pl.pallas_call(
        _merge_kernel,
        grid=(n // tm,),
        in_specs=[row(SB_WIDTH), row(DA_WIDTH), row(D_MODEL), row(D_MODEL), row(D_MODEL),
                  full(wa), full(wb), full(wo), full(g_ffn), full(wr_hi), full(wr_lo),
                  full(b_r), full(tri)],
        out_specs=[row(D_MODEL), row(D_MODEL), row(LANES), row(LANES),
                   pl.BlockSpec((1, LANES), lambda i: (0, 0))],
        out_shape=[jax.ShapeDtypeStruct((n, D_MODEL), F32),
                   jax.ShapeDtypeStruct((n, D_MODEL), F32),
                   jax.ShapeDtypeStruct((n, LANES), I32),
                   jax.ShapeDtypeStruct((n, LANES), F32),
                   jax.ShapeDtypeStruct((1, LANES), I32)],
        scratch_shapes=[pltpu.VMEM((1, LANES), F32)],
        compiler_params=_cparams(("arbitrary",)),
        name="merge_route",
    )(ya, yb, ga, gb, x2, wa, wb, wo, g_ffn, wr_hi, wr_lo, b_r, tri)


def _plan_kernel(cnt_ref, pst_ref, blk_ref, used_ref, pend_ref, *, n_blk):
    def seg(e, acc):
        pst_ref[e] = acc
        padded = ((cnt_ref[e] + (MOE_BLOCK - 1)) // MOE_BLOCK) * MOE_BLOCK
        pend_ref[e] = acc + padded
        return acc + padded

    total = lax.fori_loop(0, N_EXPERTS, seg, jnp.int32(0))
    used_ref[0] = total // MOE_BLOCK

    def blk(b, e):
        e = lax.while_loop(
            lambda ee: (ee < N_EXPERTS - 1) & (pend_ref[ee] <= b * MOE_BLOCK),
            lambda ee: ee + 1, e)
        blk_ref[b] = e
        return e

    lax.fori_loop(0, n_blk, blk, jnp.int32(0))


def _plan(counts, n_blk):
    smem = pl.BlockSpec(memory_space=pltpu.SMEM)
    return pl.pallas_call(
        functools.partial(_plan_kernel, n_blk=n_blk),
        in_specs=[smem],
        out_specs=[smem, smem, smem],
        out_shape=[jax.ShapeDtypeStruct((N_EXPERTS,), I32),
                   jax.ShapeDtypeStruct((n_blk,), I32),
                   jax.ShapeDtypeStruct((1,), I32)],
        scratch_shapes=[pltpu.SMEM((N_EXPERTS,), I32)],
        name="moe_plan",
    )(counts)


def _scatter_kernel(pst_ref, c1_ref, c2_ref, h_ref, xs_in_ref, xs_ref, d1_ref, d2_ref, sem,
                    *, tc, burst):
    del xs_in_ref
    base = pl.program_id(0) * tc

    def wait_burst():
        pltpu.make_async_copy(h_ref.at[pl.ds(0, 2 * burst)], xs_ref.at[pl.ds(0, 2 * burst)],
                              sem).wait()

    def do_burst(c, carry):
        def tok(j, carry2):
            t = c * burst + j
            c1 = c1_ref[t]
            c2 = c2_ref[t]
            d1 = pst_ref[c1 >> 16] + (c1 & 0xFFFF)
            d2 = pst_ref[c2 >> 16] + (c2 & 0xFFFF)
            d1_ref[t] = d1
            d2_ref[t] = d2
            src = h_ref.at[pl.ds(base + t, 1)]
            pltpu.make_async_copy(src, xs_ref.at[pl.ds(d1, 1)], sem).start()
            pltpu.make_async_copy(src, xs_ref.at[pl.ds(d2, 1)], sem).start()
            return carry2

        lax.fori_loop(0, burst, tok, 0)

        @pl.when(c > 0)
        def _():
            wait_burst()

        return carry

    lax.fori_loop(0, tc // burst, do_burst, 0)
    wait_burst()


def _scatter(pst, code1, code2, h2, xs_init, tc, burst):
    n = h2.shape[0]
    smem_full = pl.BlockSpec(memory_space=pltpu.SMEM)
    smem_blk = pl.BlockSpec((tc,), lambda i: (i,), memory_space=pltpu.SMEM)
    anyspec = pl.BlockSpec(memory_space=pl.ANY)
    return pl.pallas_call(
        functools.partial(_scatter_kernel, tc=tc, burst=burst),
        grid=(n // tc,),
        in_specs=[smem_full, smem_blk, smem_blk, anyspec, anyspec],
        out_specs=[anyspec, smem_blk, smem_blk],
        out_shape=[jax.ShapeDtypeStruct(xs_init.shape, xs_init.dtype),
                   jax.ShapeDtypeStruct((n,), I32),
                   jax.ShapeDtypeStruct((n,), I32)],
        scratch_shapes=[pltpu.SemaphoreType.DMA(())],
        input_output_aliases={4: 0},
        compiler_params=_cparams(("arbitrary",)),
        name="moe_scatter",
    )(pst, code1, code2, h2, xs_init)


def _expert_kernel(blk_ref, used_ref, xs_ref, wg_ref, wu_ref, wd_ref, ys_ref):
    del blk_ref
    b = pl.program_id(0)

    @pl.when(b < used_ref[0])
    def _():
        x = xs_ref[...].astype(BF16)
        gate = jnp.dot(x, wg_ref[0], preferred_element_type=F32)
        up = jnp.dot(x, wu_ref[0], preferred_element_type=F32)
        hid = (gate * jax.nn.sigmoid(gate) * up).astype(BF16)
        ys_ref[...] = jnp.dot(hid, wd_ref[0], preferred_element_type=F32)

    @pl.when(b >= used_ref[0])
    def _():
        ys_ref[...] = jnp.zeros_like(ys_ref)


def _experts(blk_e, used, xs, wg, wu, wd):
    p_rows = xs.shape[0]
    n_blk = p_rows // MOE_BLOCK
    grid_spec = pltpu.PrefetchScalarGridSpec(
        num_scalar_prefetch=2,
        grid=(n_blk,),
        in_specs=[
            pl.BlockSpec((MOE_BLOCK, D_MODEL), lambda b, blk, used: (b, 0)),
            pl.BlockSpec((1, D_MODEL, EXPERT_HIDDEN), lambda b, blk, used: (blk[b], 0, 0)),
            pl.BlockSpec((1, D_MODEL, EXPERT_HIDDEN), lambda b, blk, used: (blk[b], 0, 0)),
            pl.BlockSpec((1, EXPERT_HIDDEN, D_MODEL), lambda b, blk, used: (blk[b], 0, 0)),
        ],
        out_specs=pl.BlockSpec((MOE_BLOCK, D_MODEL), lambda b, blk, used: (b, 0)),
    )
    return pl.pallas_call(
        _expert_kernel,
        grid_spec=grid_spec,
        out_shape=jax.ShapeDtypeStruct((p_rows, D_MODEL), F32),
        compiler_params=_cparams(("arbitrary",)),
        name="moe_experts",
    )(blk_e, used, xs, wg, wu, wd)


def _combine_kernel(d1_ref, d2_ref, ys_ref, gate_ref, x1_ref, p_ref, gple_ref, wpg_ref,
                    wple_ref, gfin_ref, o_ref, buf_ref, sem, *, tm, final):
    i = pl.program_id(0)
    n_steps = pl.num_programs(0)

    def issue(step, slot):
        def tok(j, carry):
            t = step * tm + j
            pltpu.make_async_copy(ys_ref.at[pl.ds(d1_ref[t], 1)],
                                  buf_ref.at[slot, 0, pl.ds(j, 1)], sem.at[slot]).start()
            pltpu.make_async_copy(ys_ref.at[pl.ds(d2_ref[t], 1)],
                                  buf_ref.at[slot, 1, pl.ds(j, 1)], sem.at[slot]).start()
            return carry

        lax.fori_loop(0, tm, tok, 0)

    slot = i % 2

    @pl.when(i == 0)
    def _():
        issue(0, 0)

    @pl.when(i + 1 < n_steps)
    def _():
        issue(i + 1, 1 - slot)

    for half in range(2):
        pltpu.make_async_copy(ys_ref.at[pl.ds(0, tm)], buf_ref.at[slot, half],
                              sem.at[slot]).wait()

    gates = gate_ref[...]
    x2 = (x1_ref[...] + gates[:, 0:1] * buf_ref[slot, 0] + gates[:, 1:2] * buf_ref[slot, 1])
    hn = _rms(x2, gple_ref[...]).astype(BF16)
    pg = jax.nn.sigmoid(jnp.dot(hn, wpg_ref[...], preferred_element_type=F32))
    pe = jnp.dot(p_ref[...].astype(BF16), wple_ref[...], preferred_element_type=F32)
    x3 = x2 + pe * pg
    if final:
        x3 = _rms(x3, gfin_ref[...])
    o_ref[...] = x3


def _combine(d1, d2, ys, gates, x1, p2, g_ple, wpg, wple, g_fin, tm, final):
    n = x1.shape[0]
    row = lambda w: pl.BlockSpec((tm, w), lambda i, a, b: (i, 0))
    full = lambda arr: pl.BlockSpec(arr.shape, lambda i, a, b: (0,) * arr.ndim)
    grid_spec = pltpu.PrefetchScalarGridSpec(
        num_scalar_prefetch=2,
        grid=(n // tm,),
        in_specs=[pl.BlockSpec(memory_space=pl.ANY), row(LANES), row(D_MODEL), row(PLE_DIM),
                  full(g_ple), full(wpg), full(wple), full(g_fin)],
        out_specs=row(D_MODEL),
        scratch_shapes=[pltpu.VMEM((2, 2, tm, D_MODEL), F32), pltpu.SemaphoreType.DMA((2,))],
    )
    return pl.pallas_call(
        functools.partial(_combine_kernel, tm=tm, final=final),
        grid_spec=grid_spec,
        out_shape=jax.ShapeDtypeStruct((n, D_MODEL), F32),
        compiler_params=_cparams(("arbitrary",)),
        name="moe_combine_ple",
    )(d1, d2, ys, gates, x1, p2, g_ple, wpg, wple, g_fin)


def _rope_tables(positions):
    half = ROPE_DIM // 2
    inv_freq = ROPE_THETA ** (-jnp.arange(0, ROPE_DIM, 2, dtype=F32) / ROPE_DIM)
    ang = positions.astype(F32).reshape(-1, 1) * inv_freq
    cos, sin = jnp.cos(ang), jnp.sin(ang)
    n = ang.shape[0]
    ones = jnp.ones((n, HEAD_DIM - ROPE_DIM), F32)
    zeros8 = jnp.zeros((n, half), F32)
    zeros48 = jnp.zeros((n, HEAD_DIM - ROPE_DIM), F32)
    c = jnp.concatenate([cos, cos, ones], axis=1)
    s1 = jnp.concatenate([-sin, zeros8, zeros48], axis=1)
    s2 = jnp.concatenate([zeros8, sin, zeros48], axis=1)
    rep = lambda t: jnp.tile(t, (1, LANES // HEAD_DIM))
    return rep(c), rep(s1), rep(s2)


def _pick(limit, n):
    t = min(limit, n)
    assert n % t == 0, (limit, n)
    return t


def kernel(x, p, positions, g_mix, w_in, lam_q1, lam_k1, lam_q2, lam_k2, g_subln, w_br_a, w_br_b, w_o, g_ffn, w_router_group, b_router_group, w_router_expert, b_router_expert, w_exp_gate, w_exp_up, w_exp_down, g_ple, w_ple, w_ple_gate, g_final):
    b, s, d = x.shape
    depth = w_in.shape[0]
    n = b * s
    assert d == D_MODEL and w_in.shape[2] == IN_WIDTH

    tm_in = _pick(512, n)
    tq = _pick(256, s)
    tm_merge = _pick(512, n)
    tc = _pick(2048, n)
    burst = _pick(128, tc)
    tm_comb = _pick(256, n)
    assert tc % 1024 == 0 or tc == n

    n_assign = n * 2
    p_rows = ((n_assign + MOE_BLOCK - 1) // MOE_BLOCK) * MOE_BLOCK + N_EXPERTS * MOE_BLOCK
    n_blk = p_rows // MOE_BLOCK

    rope_c, rope_s1, rope_s2 = _rope_tables(positions)
    x2 = x.reshape(n, d)
    row2 = lambda v: v.reshape(1, -1)

    for i in range(depth):
        lam_init = 0.8 - 0.6 * math.exp(-0.3 * i)
        sbq, sbk, sbv, daq, dak, dav, ga, gb = _inproj(
            x2, row2(g_mix[i]), w_in[i].astype(BF16), rope_c, rope_s1, rope_s2, tm_in)
        r3 = lambda t: t.reshape(b, s, -1)
        ya = _sb_attention(r3(sbq), r3(sbk), r3(sbv), tq)
        yb = _da_attention(r3(daq), r3(dak), r3(dav), row2(lam_q1[i]), row2(lam_k1[i]),
                           row2(lam_q2[i]), row2(lam_k2[i]), row2(g_subln[i]), tq, lam_init)

        w_r = jnp.zeros((d, LANES), F32)
        w_r = w_r.at[:, :N_EXPERTS].set(w_router_expert[i])
        w_r = w_r.at[:, GROUP_LANE0:GROUP_LANE0 + N_GROUPS].set(w_router_group[i])
        wr_hi = w_r.astype(BF16)
        wr_lo = (w_r - wr_hi.astype(F32)).astype(BF16)
        b_r = jnp.zeros((1, LANES), F32)
        b_r = b_r.at[0, :N_EXPERTS].set(b_router_expert[i])
        b_r = b_r.at[0, GROUP_LANE0:GROUP_LANE0 + N_GROUPS].set(b_router_group[i])

        x1, h2, route, gates, counts = _merge(
            ya.reshape(n, -1), yb.reshape(n, -1), ga, gb, x2,
            w_br_a[i].astype(BF16), w_br_b[i].astype(BF16), w_o[i].astype(BF16),
            row2(g_ffn[i]), wr_hi, wr_lo, b_r, tm_merge)

        pst, blk_e, used = _plan(counts.reshape(LANES), n_blk)
        xs, d1, d2 = _scatter(pst, route[:, 0], route[:, 1], h2,
                              jnp.zeros((p_rows, d), F32), tc, burst)
        ys = _experts(blk_e, used, xs, w_exp_gate[i].astype(BF16), w_exp_up[i].astype(BF16),
                      w_exp_down[i].astype(BF16))
        x2 = _combine(d1, d2, ys, gates, x1, p[i].reshape(n, -1), row2(g_ple[i]),
                      w_ple_gate[i].astype(BF16), w_ple[i].astype(BF16), row2(g_final),
                      tm_comb, final=(i == depth - 1))

    return x2.reshape(b, s, d)
```

```python
import functools
import math

import numpy as np
import jax
import jax.numpy as jnp
from jax import lax
from jax.experimental import pallas as pl
from jax.experimental.pallas import tpu as pltpu

F32 = jnp.float32
BF16 = jnp.bfloat16
I32 = jnp.int32

D_MODEL = 1024
PLE_DIM = 256
RMS_EPS = 1e-6
CHUNK = 64
HEAD_DIM = 64
LANES = 128
SB_WIDTH = 512
DA_WIDTH = 512
ROPE_THETA = 500000.0
ROPE_DIM = 16
N_GROUPS = 4
EXPERTS_PER_GROUP = 8
N_EXPERTS = 32
EXPERT_HIDDEN = 256
MOE_BLOCK = 256
IN_WIDTH = 5120
OFF_SBQ, OFF_SBK, OFF_SBV, OFF_DAQ, OFF_DAK, OFF_DAV, OFF_GA, OFF_GB = (
    0, 512, 1024, 1536, 2048, 2560, 3072, 4096)
QK_SCALE = HEAD_DIM ** -0.5 * math.log2(math.e)
NEG_BIG = -1e30
GROUP_LANE0 = N_EXPERTS
VMEM_LIMIT = 56 * 1024 * 1024


def _cparams(sem):
    return pltpu.CompilerParams(dimension_semantics=sem, vmem_limit_bytes=VMEM_LIMIT)


def _rms(x, g):
    return x * lax.rsqrt(jnp.mean(x * x, axis=-1, keepdims=True) + RMS_EPS) * g


def _inproj_kernel(x_ref, g_ref, w_ref, c_ref, s1_ref, s2_ref,
                   sbq_ref, sbk_ref, sbv_ref, daq_ref, dak_ref, dav_ref, ga_ref, gb_ref):
    h = _rms(x_ref[...], g_ref[...]).astype(BF16)

    def proj(off, width):
        return jnp.dot(h, w_ref[:, off:off + width], preferred_element_type=F32)

    sbq_ref[...] = (proj(OFF_SBQ, SB_WIDTH) * QK_SCALE).astype(BF16)
    sbk_ref[...] = proj(OFF_SBK, SB_WIDTH).astype(BF16)
    sbv_ref[...] = proj(OFF_SBV, SB_WIDTH).astype(BF16)

    c = c_ref[...]
    s1 = s1_ref[...]
    s2 = s2_ref[...]

    def rope_store(t, out_ref, scale):
        for j in range(t.shape[1] // LANES):
            tc = t[:, j * LANES:(j + 1) * LANES]
            r = tc * c + pltpu.roll(tc, LANES - 8, 1) * s1 + pltpu.roll(tc, 8, 1) * s2
            out_ref[:, j * LANES:(j + 1) * LANES] = (r * scale).astype(BF16)

    rope_store(proj(OFF_DAQ, DA_WIDTH), daq_ref, QK_SCALE)
    rope_store(proj(OFF_DAK, DA_WIDTH), dak_ref, 1.0)
    dav_ref[...] = proj(OFF_DAV, DA_WIDTH).astype(BF16)
    ga_ref[...] = jax.nn.sigmoid(proj(OFF_GA, D_MODEL)).astype(BF16)
    gb_ref[...] = jax.nn.sigmoid(proj(OFF_GB, D_MODEL)).astype(BF16)


def _inproj(x2, g, w_bf, rope_c, rope_s1, rope_s2, tm):
    n = x2.shape[0]
    row = lambda w: pl.BlockSpec((tm, w), lambda i: (i, 0))
    full = lambda a: pl.BlockSpec(a.shape, lambda i: (0,) * a.ndim)
    outs = [jax.ShapeDtypeStruct((n, w), BF16)
            for w in (SB_WIDTH,) * 3 + (DA_WIDTH,) * 3 + (D_MODEL,) * 2]
    return pl.pallas_call(
        _inproj_kernel,
        grid=(n // tm,),
        in_specs=[row(D_MODEL), full(g), full(w_bf), row(LANES), row(LANES), row(LANES)],
        out_specs=[row(s.shape[1]) for s in outs],
        out_shape=outs,
        compiler_params=_cparams(("parallel",)),
        name="inproj",
    )(x2, g, w_bf, rope_c, rope_s1, rope_s2)


def _kv_block(k_ref, vt_ref, kj, tk):
    start = pl.multiple_of(kj * tk, tk)
    return k_ref[0, pl.ds(start, tk), :], vt_ref[0, kj]


def _split_lane_halves(qt_ref, tk):
    row = lax.broadcasted_iota(I32, (LANES, 1), 0)
    first = row < HEAD_DIM
    out = []
    for s in range(qt_ref.shape[2] // tk):
        qt = qt_ref[0, :, s * tk:(s + 1) * tk]
        zero = jnp.zeros_like(qt)
        out.append((jnp.where(first, qt, zero), jnp.where(first, zero, qt)))
    return out, first


def _staged(items, scores, state, mid, fin, prefetch=None):
    mids = []
    for idx, (c, vt, diag) in enumerate(items):
        payload, state[c] = mid(c, scores[idx](), vt, diag, state[c])
        if prefetch is not None:
            prefetch(c)
        mids.append(payload)
    for (c, vt, diag), payload in zip(items, mids):
        state[c] = fin(c, vt, payload, diag, state[c])
    return state


def _sweep_key_blocks(qi, nsub, tk, k_ref, vt_ref, z_ref, qk, mid, fin, state):
    state = list(state)
    nc = 2 * nsub
    items, scores = [], []
    for s_d in reversed(range(nsub)):
        k, vt = _kv_block(k_ref, vt_ref, qi * nsub + s_d, tk)
        for s in range(s_d, nsub):
            for h in range(2):
                c = 2 * s + h
                items.append((c, vt, s == s_d))
                z = qk(c, k)
                scores.append(lambda z=z: z)
    first_kj = jnp.maximum(qi * nsub - 1, 0)
    k_first, _ = _kv_block(k_ref, vt_ref, first_kj, tk)

    seen = set()

    def prefetch_first(c):
        if c not in seen:
            seen.add(c)
            z_ref[c] = qk(c, k_first)

    state = _staged(items, scores, state, mid, fin, prefetch_first)

    def body(i, st):
        kj = qi * nsub - 1 - i
        _, vt = _kv_block(k_ref, vt_ref, kj, tk)
        k_next, _ = _kv_block(k_ref, vt_ref, jnp.maximum(kj - 1, 0), tk)

        def prefetch(c):
            z_ref[c] = qk(c, k_next)

        loop_items = [(c, vt, False) for c in range(nc)]
        loop_scores = [lambda c=c: z_ref[c] for c in range(nc)]
        return tuple(_staged(loop_items, loop_scores, list(st), mid, fin, prefetch))

    return lax.fori_loop(0, qi * nsub, body, tuple(state))


def _sb_kernel(qt_ref, k_ref, vt_ref, ut_ref, o_ref, acc_ref, z_ref, *, tk):
    qi = pl.program_id(2)
    nsub = qt_ref.shape[2] // tk
    qts, first = _split_lane_halves(qt_ref, tk)
    ut = ut_ref[...]
    acc_ref[...] = jnp.zeros_like(acc_ref)
    key = lax.broadcasted_iota(I32, (tk, tk), 0)
    qry = lax.broadcasted_iota(I32, (tk, tk), 1)

    def qk(c, k):
        return jnp.dot(k, qts[c // 2][c % 2], preferred_element_type=F32)

    def mid(c, z, vt, diag, carry):
        sp = jnp.log2(1.0 + jnp.exp2(-jnp.abs(z)))
        ls = jnp.minimum(z, 0.0) - sp
        lstay = ls - z
        if diag:
            lstay = jnp.where(key < qry, lstay, 0.0)
        later = jnp.dot(ut, lstay.astype(BF16), preferred_element_type=F32)
        return (ls, later, lstay[0:1, :]), carry

    def fin(c, vt, payload, diag, carry):
        ls, later, stay0 = payload
        w = jnp.exp2(ls + later + carry)
        if diag:
            w = jnp.where(key < qry, w, 0.0)
        acc_ref[c] += jnp.dot(vt, w.astype(BF16), preferred_element_type=F32)
        return carry + later[0:1, :] + stay0

    zero = jnp.zeros((1, tk), F32)
    _sweep_key_blocks(qi, nsub, tk, k_ref, vt_ref, z_ref, qk, mid, fin, [zero] * (2 * nsub))
    for s in range(nsub):
        o_t = jnp.where(first, acc_ref[2 * s], acc_ref[2 * s + 1])
        o_ref[0, s * tk:(s + 1) * tk, :] = o_t.T.astype(o_ref.dtype)


def _attn_specs(s, tq, tk):
    qt_spec = pl.BlockSpec((1, LANES, tq), lambda bi, p, qi: (bi, p, qi))
    k_spec = pl.BlockSpec((1, s, LANES), lambda bi, p, qi: (bi, 0, p))
    vt_spec = pl.BlockSpec((1, s // tk, LANES, tk), lambda bi, p, qi: (bi, 0, p, 0))
    o_spec = pl.BlockSpec((1, tq, LANES), lambda bi, p, qi: (bi, qi, p))
    return qt_spec, k_spec, vt_spec, o_spec


def _transpose_q(q):
    return jnp.swapaxes(q, 1, 2)


def _transpose_v(v, tk):
    b, s, w = v.shape
    return jnp.swapaxes(v.reshape(b, s // tk, tk, w), 2, 3)


def _sb_attention(q, k, v, tq, tk):
    b, s, width = q.shape
    ut = jnp.asarray(np.triu(np.ones((tk, tk), np.float32), 1), BF16)
    qt_spec, k_spec, vt_spec, o_spec = _attn_specs(s, tq, tk)
    return pl.pallas_call(
        functools.partial(_sb_kernel, tk=tk),
        grid=(b, width // LANES, s // tq),
        in_specs=[qt_spec, k_spec, vt_spec, pl.BlockSpec((tk, tk), lambda bi, p, qi: (0, 0))],
        out_specs=o_spec,
        out_shape=jax.ShapeDtypeStruct((b, s, width), BF16),
        scratch_shapes=[pltpu.VMEM((2 * (tq // tk), LANES, tk), F32),
                        pltpu.VMEM((2 * (tq // tk), tk, tk), F32)],
        compiler_params=_cparams(("parallel", "parallel", "arbitrary")),
        name="sb_attn",
    )(_transpose_q(q), k, _transpose_v(v, tk), ut)


def _da_kernel(qt_ref, k_ref, vt_ref, lq1_ref, lk1_ref, lq2_ref, lk2_ref, g_ref, o_ref,
               acc_ref, z_ref, *, tk, lam_init):
    qi = pl.program_id(2)
    nsub = qt_ref.shape[2] // tk
    qts, _ = _split_lane_halves(qt_ref, tk)
    acc_ref[...] = jnp.zeros_like(acc_ref)
    key = lax.broadcasted_iota(I32, (tk, tk), 0)
    qry = lax.broadcasted_iota(I32, (tk, tk), 1)

    def qk(c, k):
        return jnp.dot(k, qts[c // 2][c % 2], preferred_element_type=F32)

    def mid(c, sc, vt, diag, st):
        mx, l = st
        if diag:
            sc = jnp.where((key // CHUNK) <= (qry // CHUNK), sc, NEG_BIG)
        m_new = jnp.maximum(mx, jnp.max(sc, axis=0, keepdims=True))
        p = jnp.exp2(sc - m_new)
        alpha = jnp.exp2(mx - m_new)
        l = alpha * l + jnp.sum(p, axis=0, keepdims=True)
        pv = jnp.dot(vt, p.astype(BF16), preferred_element_type=F32)
        return (alpha, pv), (m_new, l)

    def fin(c, vt, payload, diag, st):
        alpha, pv = payload
        acc_ref[c] = alpha * acc_ref[c] + pv
        return st

    init = (jnp.full((1, tk), NEG_BIG, F32), jnp.zeros((1, tk), F32))
    state = _sweep_key_blocks(qi, nsub, tk, k_ref, vt_ref, z_ref, qk, mid, fin,
                              [init] * (2 * nsub))

    lam = (jnp.exp(jnp.sum(lq1_ref[...] * lk1_ref[...], axis=-1, keepdims=True))
           - jnp.exp(jnp.sum(lq2_ref[...] * lk2_ref[...], axis=-1, keepdims=True))
           + lam_init)
    for s in range(nsub):
        o_t = (acc_ref[2 * s] / state[2 * s][1]
               - lam * (acc_ref[2 * s + 1] / state[2 * s + 1][1]))
        ms = jnp.mean(o_t * o_t, axis=0, keepdims=True)
        y_t = o_t * lax.rsqrt(ms + RMS_EPS) * g_ref[...] * (1.0 - lam_init)
        o_ref[0, s * tk:(s + 1) * tk, :] = y_t.T.astype(o_ref.dtype)


def _da_attention(q, k, v, lq1, lk1, lq2, lk2, g_col, tq, tk, lam_init):
    b, s, width = q.shape
    vec = lambda a: pl.BlockSpec(a.shape, lambda bi, h, qi: (0, 0))
    qt_spec, k_spec, vt_spec, o_spec = _attn_specs(s, tq, tk)
    return pl.pallas_call(
        functools.partial(_da_kernel, tk=tk, lam_init=lam_init),
        grid=(b, width // LANES, s // tq),
        in_specs=[qt_spec, k_spec, vt_spec, vec(lq1), vec(lk1), vec(lq2), vec(lk2), vec(g_col)],
        out_specs=o_spec,
        out_shape=jax.ShapeDtypeStruct((b, s, width), BF16),
        scratch_shapes=[pltpu.VMEM((2 * (tq // tk), LANES, tk), F32),
                        pltpu.VMEM((2 * (tq // tk), tk, tk), F32)],
        compiler_params=_cparams(("parallel", "parallel", "arbitrary")),
        name="da_attn",
    )(_transpose_q(q), k, _transpose_v(v, tk), lq1, lk1, lq2, lk2, g_col)


def _merge_kernel(ya_ref, yb_ref, ga_ref, gb_ref, x_ref, wa_ref, wb_ref, wo_ref, g_ref,
                  wrh_ref, wrl_ref, br_ref, tri_ref,
                  x1_ref, h2_ref, route_ref, gate_ref, cnt_ref, run_ref):
    @pl.when(pl.program_id(0) == 0)
    def _():
        run_ref[...] = jnp.zeros_like(run_ref)

    pa = jnp.dot(ya_ref[...], wa_ref[...], preferred_element_type=F32)
    pb = jnp.dot(yb_ref[...], wb_ref[...], preferred_element_type=F32)
    merged = ga_ref[...].astype(F32) * pa + gb_ref[...].astype(F32) * pb
    x1 = x_ref[...] + jnp.dot(merged.astype(BF16), wo_ref[...], preferred_element_type=F32)
    x1_ref[...] = x1
    h2 = _rms(x1, g_ref[...])
    h2_ref[...] = h2

    hh = h2.astype(BF16)
    hl = (h2 - hh.astype(F32)).astype(BF16)
    logits = (jnp.dot(hh, wrh_ref[...], preferred_element_type=F32)
              + jnp.dot(hh, wrl_ref[...], preferred_element_type=F32)
              + jnp.dot(hl, wrh_ref[...], preferred_element_type=F32)
              + br_ref[...])

    tm = logits.shape[0]
    lane = lax.broadcasted_iota(I32, (tm, LANES), 1)
    far = jnp.int32(4 * LANES)
    is_group = (lane >= GROUP_LANE0) & (lane < GROUP_LANE0 + N_GROUPS)
    gl = jnp.where(is_group, logits, -jnp.inf)
    gmax = jnp.max(gl, axis=-1, keepdims=True)
    gsel = jnp.min(jnp.where(gl == gmax, lane, far), axis=-1, keepdims=True) - GROUP_LANE0
    gsum = jnp.sum(jnp.where(is_group, jnp.exp(gl - gmax), 0.0), axis=-1, keepdims=True)
    g_p = 1.0 / gsum

    in_group = (lane >> 3) == gsel
    el = jnp.where(in_group, logits, -jnp.inf)
    m1 = jnp.max(el, axis=-1, keepdims=True)
    i1 = jnp.min(jnp.where(el == m1, lane, far), axis=-1, keepdims=True)
    el2 = jnp.where(lane == i1, -jnp.inf, el)
    m2 = jnp.max(el2, axis=-1, keepdims=True)
    i2 = jnp.min(jnp.where(el2 == m2, lane, far), axis=-1, keepdims=True)
    e2 = jnp.exp(m2 - m1)
    gate1 = g_p / (1.0 + e2)
    gate2 = gate1 * e2

    sel1 = lane == i1
    sel2 = lane == i2
    onehot = jnp.where(sel1 | sel2, 1.0, 0.0)
    before = jnp.dot(tri_ref[...], onehot.astype(BF16), preferred_element_type=F32)
    before = before + run_ref[...]
    r1 = jnp.sum(jnp.where(sel1, before, 0.0), axis=-1, keepdims=True).astype(I32)
    r2 = jnp.sum(jnp.where(sel2, before, 0.0), axis=-1, keepdims=True).astype(I32)
    run_ref[...] += jnp.sum(onehot, axis=0, keepdims=True)
    cnt_ref[...] = run_ref[...].astype(I32)

    code1 = (i1 << 16) | r1
    code2 = (i2 << 16) | r2
    route_ref[...] = jnp.where(lane == 0, code1, jnp.where(lane == 1, code2, 0))
    gate_ref[...] = jnp.where(lane == 0, gate1, jnp.where(lane == 1, gate2, 0.0))


def _merge(ya, yb, ga, gb, x2, wa, wb, wo, g_ffn, wr_hi, wr_lo, b_r, tm):
    n = x2.shape[0]
    tri = jnp.asarray(np.tril(np.ones((tm, tm), np.float32), -1), BF16)
    row = lambda w: pl.BlockSpec((tm, w), lambda i: (i, 0))
    full = lambda a: pl.BlockSpec(a.shape, lambda i: (0,) * a.ndim)
    return pl.pallas_call(
        _merge_kernel,
        grid=(n // tm,),
        in_specs=[row(SB_WIDTH), row(DA_WIDTH), row(D_MODEL), row(D_MODEL), row(D_MODEL),
                  full(wa), full(wb), full(wo), full(g_ffn), full(wr_hi), full(wr_lo),
                  full(b_r), full(tri)],
        out_specs=[row(D_MODEL), row(D_MODEL), row(LANES), row(LANES),
                   pl.BlockSpec((1, LANES), lambda i: (0, 0))],
        out_shape=[jax.ShapeDtypeStruct((n, D_MODEL), F32),
                   jax.ShapeDtypeStruct((n, D_MODEL), F32),
                   jax.ShapeDtypeStruct((n, LANES), I32),
                   jax.ShapeDtypeStruct((n, LANES), F32),
                   jax.ShapeDtypeStruct((1, LANES), I32)],
        scratch_shapes=[pltpu.VMEM((1, LANES), F32)],
        compiler_params=_cparams(("arbitrary",)),
        name="merge_route",
    )(ya, yb, ga, gb, x2, wa, wb, wo, g_ffn, wr_hi, wr_lo, b_r, tri)


def _plan_kernel(cnt_ref, pst_ref, blk_ref, used_ref, pend_ref, *, n_blk):
    def seg(e, acc):
        pst_ref[e] = acc
        padded = ((cnt_ref[e] + (MOE_BLOCK - 1)) // MOE_BLOCK) * MOE_BLOCK
        pend_ref[e] = acc + padded
        return acc + padded

    total = lax.fori_loop(0, N_EXPERTS, seg, jnp.int32(0))
    used_ref[0] = total // MOE_BLOCK

    def blk(b, e):
        e = lax.while_loop(
            lambda ee: (ee < N_EXPERTS - 1) & (pend_ref[ee] <= b * MOE_BLOCK),
            lambda ee: ee + 1, e)
        blk_ref[b] = e
        return e

    lax.fori_loop(0, n_blk, blk, jnp.int32(0))


def _plan(counts, n_blk):
    smem = pl.BlockSpec(memory_space=pltpu.SMEM)
    return pl.pallas_call(
        functools.partial(_plan_kernel, n_blk=n_blk),
        in_specs=[smem],
        out_specs=[smem, smem, smem],
        out_shape=[jax.ShapeDtypeStruct((N_EXPERTS,), I32),
                   jax.ShapeDtypeStruct((n_blk,), I32),
                   jax.ShapeDtypeStruct((1,), I32)],
        scratch_shapes=[pltpu.SMEM((N_EXPERTS,), I32)],
        name="moe_plan",
    )(counts)


def _scatter_kernel(pst_ref, c1_ref, c2_ref, h_ref, xs_in_ref, xs_ref, d1_ref, d2_ref, sem,
                    *, burst):
    del xs_in_ref
    tc = h_ref.shape[0]

    def wait_burst():
        pltpu.make_async_copy(h_ref.at[pl.ds(0, 2 * burst)], xs_ref.at[pl.ds(0, 2 * burst)],
                              sem).wait()

    def do_burst(c, carry):
        def tok(j, carry2):
            t = c * burst + j
            c1 = c1_ref[t]
            c2 = c2_ref[t]
            d1 = pst_ref[c1 >> 16] + (c1 & 0xFFFF)
            d2 = pst_ref[c2 >> 16] + (c2 & 0xFFFF)
            d1_ref[t] = d1
            d2_ref[t] = d2
            src = h_ref.at[pl.ds(t, 1)]
            pltpu.make_async_copy(src, xs_ref.at[pl.ds(d1, 1)], sem).start()
            pltpu.make_async_copy(src, xs_ref.at[pl.ds(d2, 1)], sem).start()
            return carry2

        lax.fori_loop(0, burst, tok, 0)

        @pl.when(c > 0)
        def _():
            wait_burst()

        return carry

    lax.fori_loop(0, tc // burst, do_burst, 0)
    wait_burst()


def _scatter(pst, code1, code2, h2, xs_init, tc, burst):
    n = h2.shape[0]
    smem_full = pl.BlockSpec(memory_space=pltpu.SMEM)
    smem_blk = pl.BlockSpec((tc,), lambda i: (i,), memory_space=pltpu.SMEM)
    anyspec = pl.BlockSpec(memory_space=pl.ANY)
    return pl.pallas_call(
        functools.partial(_scatter_kernel, burst=burst),
        grid=(n // tc,),
        in_specs=[smem_full, smem_blk, smem_blk,
                  pl.BlockSpec((tc, D_MODEL), lambda i: (i, 0)), anyspec],
        out_specs=[anyspec, smem_blk, smem_blk],
        out_shape=[jax.ShapeDtypeStruct(xs_init.shape, xs_init.dtype),
                   jax.ShapeDtypeStruct((n,), I32),
                   jax.ShapeDtypeStruct((n,), I32)],
        scratch_shapes=[pltpu.SemaphoreType.DMA(())],
        input_output_aliases={4: 0},
        compiler_params=_cparams(("arbitrary",)),
        name="moe_scatter",
    )(pst, code1, code2, h2, xs_init)


def _expert_kernel(blk_ref, used_ref, xs_ref, wg_ref, wu_ref, wd_ref, ys_ref):
    del blk_ref
    b = pl.program_id(0)

    @pl.when(b < used_ref[0])
    def _():
        x = xs_ref[...].astype(BF16)
        gate = jnp.dot(x, wg_ref[0], preferred_element_type=F32)
        up = jnp.dot(x, wu_ref[0], preferred_element_type=F32)
        hid = (gate * jax.nn.sigmoid(gate) * up).astype(BF16)
        ys_ref[...] = jnp.dot(hid, wd_ref[0], preferred_element_type=F32)

    @pl.when(b >= used_ref[0])
    def _():
        ys_ref[...] = jnp.zeros_like(ys_ref)


def _experts(blk_e, used, xs, wg, wu, wd):
    p_rows = xs.shape[0]
    n_blk = p_rows // MOE_BLOCK
    grid_spec = pltpu.PrefetchScalarGridSpec(
        num_scalar_prefetch=2,
        grid=(n_blk,),
        in_specs=[
            pl.BlockSpec((MOE_BLOCK, D_MODEL), lambda b, blk, used: (b, 0)),
            pl.BlockSpec((1, D_MODEL, EXPERT_HIDDEN), lambda b, blk, used: (blk[b], 0, 0)),
            pl.BlockSpec((1, D_MODEL, EXPERT_HIDDEN), lambda b, blk, used: (blk[b], 0, 0)),
            pl.BlockSpec((1, EXPERT_HIDDEN, D_MODEL), lambda b, blk, used: (blk[b], 0, 0)),
        ],
        out_specs=pl.BlockSpec((MOE_BLOCK, D_MODEL), lambda b, blk, used: (b, 0)),
    )
    return pl.pallas_call(
        _expert_kernel,
        grid_spec=grid_spec,
        out_shape=jax.ShapeDtypeStruct((p_rows, D_MODEL), F32),
        compiler_params=_cparams(("arbitrary",)),
        name="moe_experts",
    )(blk_e, used, xs, wg, wu, wd)


def _combine_kernel(d1_ref, d2_ref, ys_ref, gate_ref, x1_ref, p_ref, gple_ref, wpg_ref,
                    wple_ref, gfin_ref, o_ref, buf_ref, sem, *, tm, final):
    i = pl.program_id(0)
    n_steps = pl.num_programs(0)

    def issue(step, slot):
        def tok(j, carry):
            t = step * tm + j
            pltpu.make_async_copy(ys_ref.at[pl.ds(d1_ref[t], 1)],
                                  buf_ref.at[slot, 0, pl.ds(j, 1)], sem.at[slot]).start()
            pltpu.make_async_copy(ys_ref.at[pl.ds(d2_ref[t], 1)],
                                  buf_ref.at[slot, 1, pl.ds(j, 1)], sem.at[slot]).start()
            return carry

        lax.fori_loop(0, tm, tok, 0)

    slot = i % 2

    @pl.when(i == 0)
    def _():
        issue(0, 0)

    @pl.when(i + 1 < n_steps)
    def _():
        issue(i + 1, 1 - slot)

    for half in range(2):
        pltpu.make_async_copy(ys_ref.at[pl.ds(0, tm)], buf_ref.at[slot, half],
                              sem.at[slot]).wait()

    gates = gate_ref[...]
    x2 = (x1_ref[...] + gates[:, 0:1] * buf_ref[slot, 0] + gates[:, 1:2] * buf_ref[slot, 1])
    hn = _rms(x2, gple_ref[...]).astype(BF16)
    pg = jax.nn.sigmoid(jnp.dot(hn, wpg_ref[...], preferred_element_type=F32))
    pe = jnp.dot(p_ref[...].astype(BF16), wple_ref[...], preferred_element_type=F32)
    x3 = x2 + pe * pg
    if final:
        x3 = _rms(x3, gfin_ref[...])
    o_ref[...] = x3


def _combine(d1, d2, ys, gates, x1, p2, g_ple, wpg, wple, g_fin, tm, final):
    n = x1.shape[0]
    row = lambda w: pl.BlockSpec((tm, w), lambda i, a, b: (i, 0))
    full = lambda arr: pl.BlockSpec(arr.shape, lambda i, a, b: (0,) * arr.ndim)
    grid_spec = pltpu.PrefetchScalarGridSpec(
        num_scalar_prefetch=2,
        grid=(n // tm,),
        in_specs=[pl.BlockSpec(memory_space=pl.ANY), row(LANES), row(D_MODEL), row(PLE_DIM),
                  full(g_ple), full(wpg), full(wple), full(g_fin)],
        out_specs=row(D_MODEL),
        scratch_shapes=[pltpu.VMEM((2, 2, tm, D_MODEL), F32), pltpu.SemaphoreType.DMA((2,))],
    )
    return pl.pallas_call(
        functools.partial(_combine_kernel, tm=tm, final=final),
        grid_spec=grid_spec,
        out_shape=jax.ShapeDtypeStruct((n, D_MODEL), F32),
        compiler_params=_cparams(("arbitrary",)),
        name="moe_combine_ple",
    )(d1, d2, ys, gates, x1, p2, g_ple, wpg, wple, g_fin)


def _rope_tables(positions):
    half = ROPE_DIM // 2
    inv_freq = ROPE_THETA ** (-jnp.arange(0, ROPE_DIM, 2, dtype=F32) / ROPE_DIM)
    ang = positions.astype(F32).reshape(-1, 1) * inv_freq
    cos, sin = jnp.cos(ang), jnp.sin(ang)
    n = ang.shape[0]
    ones = jnp.ones((n, HEAD_DIM - ROPE_DIM), F32)
    zeros8 = jnp.zeros((n, half), F32)
    zeros48 = jnp.zeros((n, HEAD_DIM - ROPE_DIM), F32)
    c = jnp.concatenate([cos, cos, ones], axis=1)
    s1 = jnp.concatenate([-sin, zeros8, zeros48], axis=1)
    s2 = jnp.concatenate([zeros8, sin, zeros48], axis=1)
    rep = lambda t: jnp.tile(t, (1, LANES // HEAD_DIM))
    return rep(c), rep(s1), rep(s2)


def _pick(limit, n):
    t = min(limit, n)
    assert n % t == 0, (limit, n)
    return t


def kernel(x, p, positions, g_mix, w_in, lam_q1, lam_k1, lam_q2, lam_k2, g_subln, w_br_a, w_br_b, w_o, g_ffn, w_router_group, b_router_group, w_router_expert, b_router_expert, w_exp_gate, w_exp_up, w_exp_down, g_ple, w_ple, w_ple_gate, g_final):
    b, s, d = x.shape
    depth = w_in.shape[0]
    n = b * s
    assert d == D_MODEL and w_in.shape[2] == IN_WIDTH

    tm_in = _pick(512, n)
    tk = _pick(256, s)
    tq = _pick(2 * tk, s)
    tm_merge = _pick(512, n)
    tc = _pick(1024, n)
    burst = _pick(128, tc)
    tm_comb = _pick(256, n)
    assert tc % 1024 == 0 or tc == n

    n_assign = n * 2
    p_rows = ((n_assign + MOE_BLOCK - 1) // MOE_BLOCK) * MOE_BLOCK + N_EXPERTS * MOE_BLOCK
    n_blk = p_rows // MOE_BLOCK

    rope_c, rope_s1, rope_s2 = _rope_tables(positions)
    x2 = x.reshape(n, d)
    row2 = lambda v: v.reshape(1, -1)

    for i in range(depth):
        lam_init = 0.8 - 0.6 * math.exp(-0.3 * i)
        sbq, sbk, sbv, daq, dak, dav, ga, gb = _inproj(
            x2, row2(g_mix[i]), w_in[i].astype(BF16), rope_c, rope_s1, rope_s2, tm_in)
        r3 = lambda t: t.reshape(b, s, -1)
        ya = _sb_attention(r3(sbq), r3(sbk), r3(sbv), tq, tk)
        yb = _da_attention(r3(daq), r3(dak), r3(dav), row2(lam_q1[i]), row2(lam_k1[i]),
                           row2(lam_q2[i]), row2(lam_k2[i]), g_subln[i].reshape(-1, 1),
                           tq, tk, lam_init)

        w_r = jnp.zeros((d, LANES), F32)
        w_r = w_r.at[:, :N_EXPERTS].set(w_router_expert[i])
        w_r = w_r.at[:, GROUP_LANE0:GROUP_LANE0 + N_GROUPS].set(w_router_group[i])
        wr_hi = w_r.astype(BF16)
        wr_lo = (w_r - wr_hi.astype(F32)).astype(BF16)
        b_r = jnp.zeros((1, LANES), F32)
        b_r = b_r.at[0, :N_EXPERTS].set(b_router_expert[i])
        b_r = b_r.at[0, GROUP_LANE0:GROUP_LANE0 + N_GROUPS].set(b_router_group[i])

        x1, h2, route, gates, counts = _merge(
            ya.reshape(n, -1), yb.reshape(n, -1), ga, gb, x2,
            w_br_a[i].astype(BF16), w_br_b[i].astype(BF16), w_o[i].astype(BF16),
            row2(g_ffn[i]), wr_hi, wr_lo, b_r, tm_merge)

        pst, blk_e, used = _plan(counts.reshape(LANES), n_blk)
        xs, d1, d2 = _scatter(pst, route[:, 0], route[:, 1], h2,
                              jnp.zeros((p_rows, d), F32), tc, burst)
        ys = _experts(blk_e, used, xs, w_exp_gate[i].astype(BF16), w_exp_up[i].astype(BF16),
                      w_exp_down[i].astype(BF16))
        x2 = _combine(d1, d2, ys, gates, x1, p[i].reshape(n, -1), row2(g_ple[i]),
                      w_ple_gate[i].astype(BF16), w_ple[i].astype(BF16), row2(g_final),
                      tm_comb, final=(i == depth - 1))

    return x2.reshape(b, s, d)
```

```python
import functools
import math

import numpy as np
import jax
import jax.numpy as jnp
from jax import lax
from jax.experimental import pallas as pl
from jax.experimental.pallas import tpu as pltpu

F32 = jnp.float32
BF16 = jnp.bfloat16
I32 = jnp.int32

D_MODEL = 1024
PLE_DIM = 256
RMS_EPS = 1e-6
CHUNK = 64
HEAD_DIM = 64
LANES = 128
SB_WIDTH = 512
DA_WIDTH = 512
ROPE_THETA = 500000.0
ROPE_DIM = 16
N_GROUPS = 4
EXPERTS_PER_GROUP = 8
N_EXPERTS = 32
EXPERT_HIDDEN = 256
MOE_BLOCK = 256
IN_WIDTH = 5120
OFF_SBQ, OFF_SBK, OFF_SBV, OFF_DAQ, OFF_DAK, OFF_DAV, OFF_GA, OFF_GB = (
    0, 512, 1024, 1536, 2048, 2560, 3072, 4096)
QK_SCALE = HEAD_DIM ** -0.5 * math.log2(math.e)
NEG_BIG = -1e30
UNDERFLOW_LOG2 = -160.0
GROUP_LANE0 = N_EXPERTS
VMEM_LIMIT = 56 * 1024 * 1024
ISSUE_UNROLL = 8


def _cparams(sem):
    return pltpu.CompilerParams(dimension_semantics=sem, vmem_limit_bytes=VMEM_LIMIT)


def _rms(x, g):
    return x * lax.rsqrt(jnp.mean(x * x, axis=-1, keepdims=True) + RMS_EPS) * g


def _inproj_kernel(x_ref, g_ref, w_ref, c_ref, s1_ref, s2_ref,
                   sbq_ref, sbk_ref, sbv_ref, daq_ref, dak_ref, dav_ref, ga_ref, gb_ref):
    h = _rms(x_ref[...], g_ref[...]).astype(BF16)

    def proj(off, width):
        return jnp.dot(h, w_ref[:, off:off + width], preferred_element_type=F32)

    sbq_ref[...] = (proj(OFF_SBQ, SB_WIDTH) * QK_SCALE).astype(BF16)
    sbk_ref[...] = proj(OFF_SBK, SB_WIDTH).astype(BF16)
    sbv_ref[...] = proj(OFF_SBV, SB_WIDTH).astype(BF16)

    c = c_ref[...]
    s1 = s1_ref[...]
    s2 = s2_ref[...]

    def rope_store(t, out_ref, scale):
        for j in range(t.shape[1] // LANES):
            tc = t[:, j * LANES:(j + 1) * LANES]
            r = tc * c + pltpu.roll(tc, LANES - 8, 1) * s1 + pltpu.roll(tc, 8, 1) * s2
            out_ref[:, j * LANES:(j + 1) * LANES] = (r * scale).astype(BF16)

    rope_store(proj(OFF_DAQ, DA_WIDTH), daq_ref, QK_SCALE)
    rope_store(proj(OFF_DAK, DA_WIDTH), dak_ref, 1.0)
    dav_ref[...] = proj(OFF_DAV, DA_WIDTH).astype(BF16)
    ga_ref[...] = jax.nn.sigmoid(proj(OFF_GA, D_MODEL)).astype(BF16)
    gb_ref[...] = jax.nn.sigmoid(proj(OFF_GB, D_MODEL)).astype(BF16)


def _inproj(x2, g, w_bf, rope_c, rope_s1, rope_s2, tm):
    n = x2.shape[0]
    row = lambda w: pl.BlockSpec((tm, w), lambda i: (i, 0))
    full = lambda a: pl.BlockSpec(a.shape, lambda i: (0,) * a.ndim)
    outs = [jax.ShapeDtypeStruct((n, w), BF16)
            for w in (SB_WIDTH,) * 3 + (DA_WIDTH,) * 3 + (D_MODEL,) * 2]
    return pl.pallas_call(
        _inproj_kernel,
        grid=(n // tm,),
        in_specs=[row(D_MODEL), full(g), full(w_bf), row(LANES), row(LANES), row(LANES)],
        out_specs=[row(s.shape[1]) for s in outs],
        out_shape=outs,
        compiler_params=_cparams(("parallel",)),
        name="inproj",
    )(x2, g, w_bf, rope_c, rope_s1, rope_s2)


def _kv_block(k_ref, vt_ref, kj, tk):
    start = pl.multiple_of(kj * tk, tk)
    return k_ref[0, pl.ds(start, tk), :], vt_ref[0, kj]


def _split_lane_halves(qt_ref, tk):
    row = lax.broadcasted_iota(I32, (LANES, 1), 0)
    first = row < HEAD_DIM
    out = []
    for s in range(qt_ref.shape[2] // tk):
        qt = qt_ref[0, :, s * tk:(s + 1) * tk]
        zero = jnp.zeros_like(qt)
        out.append((jnp.where(first, qt, zero), jnp.where(first, zero, qt)))
    return out, first


def _staged(items, scores, state, mid, fin, prefetch=None):
    mids = []
    for idx, (c, vt, diag) in enumerate(items):
        payload, state[c] = mid(c, scores[idx](), vt, diag, state[c])
        if prefetch is not None:
            prefetch(c)
        mids.append(payload)
    for (c, vt, diag), payload in zip(items, mids):
        state[c] = fin(c, vt, payload, diag, state[c])
    return state


def _sweep_key_blocks(qi, nsub, tk, k_ref, vt_ref, z_ref, qk, mid, fin, state, alive=None):
    state = list(state)
    nc = 2 * nsub
    items, scores = [], []
    for s_d in reversed(range(nsub)):
        k, vt = _kv_block(k_ref, vt_ref, qi * nsub + s_d, tk)
        for s in range(s_d, nsub):
            for h in range(2):
                c = 2 * s + h
                items.append((c, vt, s == s_d))
                z = qk(c, k)
                scores.append(lambda z=z: z)
    first_kj = jnp.maximum(qi * nsub - 1, 0)
    k_first, _ = _kv_block(k_ref, vt_ref, first_kj, tk)

    seen = set()

    def prefetch_first(c):
        if c not in seen:
            seen.add(c)
            z_ref[c] = qk(c, k_first)

    state = _staged(items, scores, state, mid, fin, prefetch_first)

    def body(i, st):
        kj = qi * nsub - 1 - i
        _, vt = _kv_block(k_ref, vt_ref, kj, tk)
        k_next, _ = _kv_block(k_ref, vt_ref, jnp.maximum(kj - 1, 0), tk)

        def prefetch(c):
            z_ref[c] = qk(c, k_next)

        loop_items = [(c, vt, False) for c in range(nc)]
        loop_scores = [lambda c=c: z_ref[c] for c in range(nc)]
        return tuple(_staged(loop_items, loop_scores, list(st), mid, fin, prefetch))

    n_loop = qi * nsub
    if alive is None:
        return lax.fori_loop(0, n_loop, body, tuple(state))

    def cond(carry):
        i, go, _ = carry
        return (i < n_loop) & go

    def step(carry):
        i, _, st = carry
        st = body(i, st)
        return i + 1, alive(st), st

    return lax.while_loop(cond, step, (jnp.int32(0), alive(state), tuple(state)))[2]


def _sb_kernel(qt_ref, k_ref, vt_ref, ut_ref, o_ref, acc_ref, z_ref, *, tk):
    qi = pl.program_id(2)
    nsub = qt_ref.shape[2] // tk
    qts, first = _split_lane_halves(qt_ref, tk)
    ut = ut_ref[...]
    acc_ref[...] = jnp.zeros_like(acc_ref)
    key = lax.broadcasted_iota(I32, (tk, tk), 0)
    qry = lax.broadcasted_iota(I32, (tk, tk), 1)

    def qk(c, k):
        return jnp.dot(k, qts[c // 2][c % 2], preferred_element_type=F32)

    def mid(c, z, vt, diag, carry):
        sp = jnp.log2(1.0 + jnp.exp2(-jnp.abs(z)))
        ls = jnp.minimum(z, 0.0) - sp
        lstay = ls - z
        if diag:
            lstay = jnp.where(key < qry, lstay, 0.0)
        later = jnp.dot(ut, lstay.astype(BF16), preferred_element_type=F32)
        return (ls, later, lstay[0:1, :]), carry

    def fin(c, vt, payload, diag, carry):
        ls, later, stay0 = payload
        w = jnp.exp2(ls + later + carry)
        if diag:
            w = jnp.where(key < qry, w, 0.0)
        acc_ref[c] += jnp.dot(vt, w.astype(BF16), preferred_element_type=F32)
        return carry + later[0:1, :] + stay0

    def alive(carries):
        top = functools.reduce(jnp.maximum, carries)
        return jnp.max(top) >= UNDERFLOW_LOG2

    zero = jnp.zeros((1, tk), F32)
    _sweep_key_blocks(qi, nsub, tk, k_ref, vt_ref, z_ref, qk, mid, fin, [zero] * (2 * nsub),
                      alive)
    for s in range(nsub):
        o_t = jnp.where(first, acc_ref[2 * s], acc_ref[2 * s + 1])
        o_ref[0, s * tk:(s + 1) * tk, :] = o_t.T.astype(o_ref.dtype)


def _attn_specs(s, tq, tk):
    qt_spec = pl.BlockSpec((1, LANES, tq), lambda bi, p, qi: (bi, p, qi))
    k_spec = pl.BlockSpec((1, s, LANES), lambda bi, p, qi: (bi, 0, p))
    vt_spec = pl.BlockSpec((1, s // tk, LANES, tk), lambda bi, p, qi: (bi, 0, p, 0))
    o_spec = pl.BlockSpec((1, tq, LANES), lambda bi, p, qi: (bi, qi, p))
    return qt_spec, k_spec, vt_spec, o_spec


def _transpose_q(q):
    return jnp.swapaxes(q, 1, 2)


def _transpose_v(v, tk):
    b, s, w = v.shape
    return jnp.swapaxes(v.reshape(b, s // tk, tk, w), 2, 3)


def _sb_attention(q, k, v, tq, tk):
    b, s, width = q.shape
    ut = jnp.asarray(np.triu(np.ones((tk, tk), np.float32), 1), BF16)
    qt_spec, k_spec, vt_spec, o_spec = _attn_specs(s, tq, tk)
    return pl.pallas_call(
        functools.partial(_sb_kernel, tk=tk),
        grid=(b, width // LANES, s // tq),
        in_specs=[qt_spec, k_spec, vt_spec, pl.BlockSpec((tk, tk), lambda bi, p, qi: (0, 0))],
        out_specs=o_spec,
        out_shape=jax.ShapeDtypeStruct((b, s, width), BF16),
        scratch_shapes=[pltpu.VMEM((2 * (tq // tk), LANES, tk), F32),
                        pltpu.VMEM((2 * (tq // tk), tk, tk), F32)],
        compiler_params=_cparams(("parallel", "parallel", "arbitrary")),
        name="sb_attn",
    )(_transpose_q(q), k, _transpose_v(v, tk), ut)


def _da_kernel(qt_ref, k_ref, vt_ref, lq1_ref, lk1_ref, lq2_ref, lk2_ref, g_ref, o_ref,
               acc_ref, z_ref, *, tk, lam_init):
    qi = pl.program_id(2)
    nsub = qt_ref.shape[2] // tk
    qts, _ = _split_lane_halves(qt_ref, tk)
    acc_ref[...] = jnp.zeros_like(acc_ref)
    key = lax.broadcasted_iota(I32, (tk, tk), 0)
    qry = lax.broadcasted_iota(I32, (tk, tk), 1)

    def qk(c, k):
        return jnp.dot(k, qts[c // 2][c % 2], preferred_element_type=F32)

    def mid(c, sc, vt, diag, st):
        mx, l = st
        if diag:
            sc = jnp.where((key // CHUNK) <= (qry // CHUNK), sc, NEG_BIG)
        m_new = jnp.maximum(mx, jnp.max(sc, axis=0, keepdims=True))
        p = jnp.exp2(sc - m_new)
        alpha = jnp.exp2(mx - m_new)
        l = alpha * l + jnp.sum(p, axis=0, keepdims=True)
        pv = jnp.dot(vt, p.astype(BF16), preferred_element_type=F32)
        return (alpha, pv), (m_new, l)

    def fin(c, vt, payload, diag, st):
        alpha, pv = payload
        acc_ref[c] = alpha * acc_ref[c] + pv
        return st

    init = (jnp.full((1, tk), NEG_BIG, F32), jnp.zeros((1, tk), F32))
    state = _sweep_key_blocks(qi, nsub, tk, k_ref, vt_ref, z_ref, qk, mid, fin,
                              [init] * (2 * nsub))

    lam = (jnp.exp(jnp.sum(lq1_ref[...] * lk1_ref[...], axis=-1, keepdims=True))
           - jnp.exp(jnp.sum(lq2_ref[...] * lk2_ref[...], axis=-1, keepdims=True))
           + lam_init)
    for s in range(nsub):
        o_t = (acc_ref[2 * s] / state[2 * s][1]
               - lam * (acc_ref[2 * s + 1] / state[2 * s + 1][1]))
        ms = jnp.mean(o_t * o_t, axis=0, keepdims=True)
        y_t = o_t * lax.rsqrt(ms + RMS_EPS) * g_ref[...] * (1.0 - lam_init)
        o_ref[0, s * tk:(s + 1) * tk, :] = y_t.T.astype(o_ref.dtype)


def _da_attention(q, k, v, lq1, lk1, lq2, lk2, g_col, tq, tk, lam_init):
    b, s, width = q.shape
    vec = lambda a: pl.BlockSpec(a.shape, lambda bi, h, qi: (0, 0))
    qt_spec, k_spec, vt_spec, o_spec = _attn_specs(s, tq, tk)
    return pl.pallas_call(
        functools.partial(_da_kernel, tk=tk, lam_init=lam_init),
        grid=(b, width // LANES, s // tq),
        in_specs=[qt_spec, k_spec, vt_spec, vec(lq1), vec(lk1), vec(lq2), vec(lk2), vec(g_col)],
        out_specs=o_spec,
        out_shape=jax.ShapeDtypeStruct((b, s, width), BF16),
        scratch_shapes=[pltpu.VMEM((2 * (tq // tk), LANES, tk), F32),
                        pltpu.VMEM((2 * (tq // tk), tk, tk), F32)],
        compiler_params=_cparams(("parallel", "parallel", "arbitrary")),
        name="da_attn",
    )(_transpose_q(q), k, _transpose_v(v, tk), lq1, lk1, lq2, lk2, g_col)


def _merge_kernel(ya_ref, yb_ref, ga_ref, gb_ref, x_ref, wa_ref, wb_ref, wo_ref, g_ref,
                  wrh_ref, wrl_ref, br_ref, tri_ref,
                  x1_ref, h2_ref, route_ref, gate_ref, cnt_ref, run_ref):
    @pl.when(pl.program_id(0) == 0)
    def _():
        run_ref[...] = jnp.zeros_like(run_ref)

    pa = jnp.dot(ya_ref[...], wa_ref[...], preferred_element_type=F32)
    pb = jnp.dot(yb_ref[...], wb_ref[...], preferred_element_type=F32)
    merged = ga_ref[...].astype(F32) * pa + gb_ref[...].astype(F32) * pb
    x1 = x_ref[...] + jnp.dot(merged.astype(BF16), wo_ref[...], preferred_element_type=F32)
    x1_ref[...] = x1
    h2 = _rms(x1, g_ref[...])
    h2_ref[...] = h2

    hh = h2.astype(BF16)
    hl = (h2 - hh.astype(F32)).astype(BF16)
    logits = (jnp.dot(hh, wrh_ref[...], preferred_element_type=F32)
              + jnp.dot(hh, wrl_ref[...], preferred_element_type=F32)
              + jnp.dot(hl, wrh_ref[...], preferred_element_type=F32)
              + br_ref[...])

    tm = logits.shape[0]
    lane = lax.broadcasted_iota(I32, (tm, LANES), 1)
    far = jnp.int32(4 * LANES)
    is_group = (lane >= GROUP_LANE0) & (lane < GROUP_LANE0 + N_GROUPS)
    gl = jnp.where(is_group, logits, -jnp.inf)
    gmax = jnp.max(gl, axis=-1, keepdims=True)
    gsel = jnp.min(jnp.where(gl == gmax, lane, far), axis=-1, keepdims=True) - GROUP_LANE0
    gsum = jnp.sum(jnp.where(is_group, jnp.exp(gl - gmax), 0.0), axis=-1, keepdims=True)
    g_p = 1.0 / gsum

    in_group = (lane >> 3) == gsel
    el = jnp.where(in_group, logits, -jnp.inf)
    m1 = jnp.max(el, axis=-1, keepdims=True)
    i1 = jnp.min(jnp.where(el == m1, lane, far), axis=-1, keepdims=True)
    el2 = jnp.where(lane == i1, -jnp.inf, el)
    m2 = jnp.max(el2, axis=-1, keepdims=True)
    i2 = jnp.min(jnp.where(el2 == m2, lane, far), axis=-1, keepdims=True)
    e2 = jnp.exp(m2 - m1)
    gate1 = g_p / (1.0 + e2)
    gate2 = gate1 * e2

    sel1 = lane == i1
    sel2 = lane == i2
    onehot = jnp.where(sel1 | sel2, 1.0, 0.0)
    before = jnp.dot(tri_ref[...], onehot.astype(BF16), preferred_element_type=F32)
    before = before + run_ref[...]
    r1 = jnp.sum(jnp.where(sel1, before, 0.0), axis=-1, keepdims=True).astype(I32)
    r2 = jnp.sum(jnp.where(sel2, before, 0.0), axis=-1, keepdims=True).astype(I32)
    run_ref[...] += jnp.sum(onehot, axis=0, keepdims=True)
    cnt_ref[...] = run_ref[...].astype(I32)

    code1 = (i1 << 16) | r1
    code2 = (i2 << 16) | r2
    route_ref[...] = jnp.where(lane == 0, code1, jnp.where(lane == 1, code2, 0))
    gate_ref[...] = jnp.where(lane == 0, gate1, jnp.where(lane == 1, gate2, 0.0))


def _merge(ya, yb, ga, gb, x2, wa, wb, wo, g_ffn, wr_hi, wr_lo, b_r, tm):
    n = x2.shape[0]
    tri = jnp.asarray(np.tril(np.ones((tm, tm), np.float32), -1), BF16)
    row = lambda w: pl.BlockSpec((tm, w), lambda i: (i, 0))
    full = lambda a: pl.BlockSpec(a.shape, lambda i: (0,) * a.ndim)
    return pl.pallas_call(
        _merge_kernel,
        grid=(n // tm,),
        in_specs=[row(SB_WIDTH), row(DA_WIDTH), row(D_MODEL), row(D_MODEL), row(D_MODEL),
                  full(wa), full(wb), full(wo), full(g_ffn), full(wr_hi), full(wr_lo),
                  full(b_r), full(tri)],
        out_specs=[row(D_MODEL), row(D_MODEL), row(LANES), row(LANES),
                   pl.BlockSpec((1, LANES), lambda i: (0, 0))],
        out_shape=[jax.ShapeDtypeStruct((n, D_MODEL), F32),
                   jax.ShapeDtypeStruct((n, D_MODEL), F32),
                   jax.ShapeDtypeStruct((n, LANES), I32),
                   jax.ShapeDtypeStruct((n, LANES), F32),
                   jax.ShapeDtypeStruct((1, LANES), I32)],
        scratch_shapes=[pltpu.VMEM((1, LANES), F32)],
        compiler_params=_cparams(("arbitrary",)),
        name="merge_route",
    )(ya, yb, ga, gb, x2, wa, wb, wo, g_ffn, wr_hi, wr_lo, b_r, tri)


def _plan_kernel(cnt_ref, pst_ref, blk_ref, used_ref, pend_ref, *, n_blk):
    def seg(e, acc):
        pst_ref[e] = acc
        padded = ((cnt_ref[e] + (MOE_BLOCK - 1)) // MOE_BLOCK) * MOE_BLOCK
        pend_ref[e] = acc + padded
        return acc + padded

    total = lax.fori_loop(0, N_EXPERTS, seg, jnp.int32(0))
    used_ref[0] = total // MOE_BLOCK

    def blk(b, e):
        e = lax.while_loop(
            lambda ee: (ee < N_EXPERTS - 1) & (pend_ref[ee] <= b * MOE_BLOCK),
            lambda ee: ee + 1, e)
        blk_ref[b] = e
        return e

    lax.fori_loop(0, n_blk, blk, jnp.int32(0))


def _plan(counts, n_blk):
    smem = pl.BlockSpec(memory_space=pltpu.SMEM)
    return pl.pallas_call(
        functools.partial(_plan_kernel, n_blk=n_blk),
        in_specs=[smem],
        out_specs=[smem, smem, smem],
        out_shape=[jax.ShapeDtypeStruct((N_EXPERTS,), I32),
                   jax.ShapeDtypeStruct((n_blk,), I32),
                   jax.ShapeDtypeStruct((1,), I32)],
        scratch_shapes=[pltpu.SMEM((N_EXPERTS,), I32)],
        name="moe_plan",
    )(counts)


def _scatter_kernel(pst_ref, c1_ref, c2_ref, h_ref, xs_in_ref, xs_ref, d1_ref, d2_ref, sem,
                    *, burst):
    del xs_in_ref
    tc = h_ref.shape[0]

    def wait_burst():
        pltpu.make_async_copy(h_ref.at[pl.ds(0, 2 * burst)], xs_ref.at[pl.ds(0, 2 * burst)],
                              sem).wait()

    def do_burst(c, carry):
        def tok(j, carry2):
            t = c * burst + j
            c1 = c1_ref[t]
            c2 = c2_ref[t]
            d1 = pst_ref[c1 >> 16] + (c1 & 0xFFFF)
            d2 = pst_ref[c2 >> 16] + (c2 & 0xFFFF)
            d1_ref[t] = d1
            d2_ref[t] = d2
            src = h_ref.at[pl.ds(t, 1)]
            pltpu.make_async_copy(src, xs_ref.at[pl.ds(d1, 1)], sem).start()
            pltpu.make_async_copy(src, xs_ref.at[pl.ds(d2, 1)], sem).start()
            return carry2

        lax.fori_loop(0, burst, tok, 0, unroll=ISSUE_UNROLL)

        @pl.when(c > 0)
        def _():
            wait_burst()

        return carry

    lax.fori_loop(0, tc // burst, do_burst, 0)
    wait_burst()


def _scatter(pst, code1, code2, h2, xs_init, tc, burst):
    n = h2.shape[0]
    smem_full = pl.BlockSpec(memory_space=pltpu.SMEM)
    smem_blk = pl.BlockSpec((tc,), lambda i: (i,), memory_space=pltpu.SMEM)
    anyspec = pl.BlockSpec(memory_space=pl.ANY)
    return pl.pallas_call(
        functools.partial(_scatter_kernel, burst=burst),
        grid=(n // tc,),
        in_specs=[smem_full, smem_blk, smem_blk,
                  pl.BlockSpec((tc, D_MODEL), lambda i: (i, 0)), anyspec],
        out_specs=[anyspec, smem_blk, smem_blk],
        out_shape=[jax.ShapeDtypeStruct(xs_init.shape, xs_init.dtype),
                   jax.ShapeDtypeStruct((n,), I32),
                   jax.ShapeDtypeStruct((n,), I32)],
        scratch_shapes=[pltpu.SemaphoreType.DMA(())],
        input_output_aliases={4: 0},
        compiler_params=_cparams(("arbitrary",)),
        name="moe_scatter",
    )(pst, code1, code2, h2, xs_init)


def _expert_kernel(blk_ref, used_ref, xs_ref, wg_ref, wu_ref, wd_ref, ys_ref):
    del blk_ref
    b = pl.program_id(0)

    @pl.when(b < used_ref[0])
    def _():
        x = xs_ref[...].astype(BF16)
        gate = jnp.dot(x, wg_ref[0], preferred_element_type=F32)
        up = jnp.dot(x, wu_ref[0], preferred_element_type=F32)
        hid = (gate * jax.nn.sigmoid(gate) * up).astype(BF16)
        ys_ref[...] = jnp.dot(hid, wd_ref[0], preferred_element_type=F32)

    @pl.when(b >= used_ref[0])
    def _():
        ys_ref[...] = jnp.zeros_like(ys_ref)


def _experts(blk_e, used, xs, wg, wu, wd):
    p_rows = xs.shape[0]
    n_blk = p_rows // MOE_BLOCK
    grid_spec = pltpu.PrefetchScalarGridSpec(
        num_scalar_prefetch=2,
        grid=(n_blk,),
        in_specs=[
            pl.BlockSpec((MOE_BLOCK, D_MODEL), lambda b, blk, used: (b, 0)),
            pl.BlockSpec((1, D_MODEL, EXPERT_HIDDEN), lambda b, blk, used: (blk[b], 0, 0)),
            pl.BlockSpec((1, D_MODEL, EXPERT_HIDDEN), lambda b, blk, used: (blk[b], 0, 0)),
            pl.BlockSpec((1, EXPERT_HIDDEN, D_MODEL), lambda b, blk, used: (blk[b], 0, 0)),
        ],
        out_specs=pl.BlockSpec((MOE_BLOCK, D_MODEL), lambda b, blk, used: (b, 0)),
    )
    return pl.pallas_call(
        _expert_kernel,
        grid_spec=grid_spec,
        out_shape=jax.ShapeDtypeStruct((p_rows, D_MODEL), F32),
        compiler_params=_cparams(("arbitrary",)),
        name="moe_experts",
    )(blk_e, used, xs, wg, wu, wd)


def _combine_kernel(d1_ref, d2_ref, ys_ref, gate_ref, x1_ref, p_ref, gple_ref, wpg_ref,
                    wple_ref, gfin_ref, o_ref, buf_ref, sem, *, tm, final):
    i = pl.program_id(0)
    n_steps = pl.num_programs(0)

    def issue(step, slot):
        def tok(j, carry):
            t = step * tm + j
            pltpu.make_async_copy(ys_ref.at[pl.ds(d1_ref[t], 1)],
                                  buf_ref.at[slot, 0, pl.ds(j, 1)], sem.at[slot]).start()
            pltpu.make_async_copy(ys_ref.at[pl.ds(d2_ref[t], 1)],
                                  buf_ref.at[slot, 1, pl.ds(j, 1)], sem.at[slot]).start()
            return carry

        lax.fori_loop(0, tm, tok, 0, unroll=ISSUE_UNROLL)

    slot = i % 2

    @pl.when(i == 0)
    def _():
        issue(0, 0)

    @pl.when(i + 1 < n_steps)
    def _():
        issue(i + 1, 1 - slot)

    for half in range(2):
        pltpu.make_async_copy(ys_ref.at[pl.ds(0, tm)], buf_ref.at[slot, half],
                              sem.at[slot]).wait()

    gates = gate_ref[...]
    x2 = (x1_ref[...] + gates[:, 0:1] * buf_ref[slot, 0] + gates[:, 1:2] * buf_ref[slot, 1])
    hn = _rms(x2, gple_ref[...]).astype(BF16)
    pg = jax.nn.sigmoid(jnp.dot(hn, wpg_ref[...], preferred_element_type=F32))
    pe = jnp.dot(p_ref[...].astype(BF16), wple_ref[...], preferred_element_type=F32)
    x3 = x2 + pe * pg
    if final:
        x3 = _rms(x3, gfin_ref[...])
    o_ref[...] = x3


def _combine(d1, d2, ys, gates, x1, p2, g_ple, wpg, wple, g_fin, tm, final):
    n = x1.shape[0]
    row = lambda w: pl.BlockSpec((tm, w), lambda i, a, b: (i, 0))
    full = lambda arr: pl.BlockSpec(arr.shape, lambda i, a, b: (0,) * arr.ndim)
    grid_spec = pltpu.PrefetchScalarGridSpec(
        num_scalar_prefetch=2,
        grid=(n // tm,),
        in_specs=[pl.BlockSpec(memory_space=pl.ANY), row(LANES), row(D_MODEL), row(PLE_DIM),
                  full(g_ple), full(wpg), full(wple), full(g_fin)],
        out_specs=row(D_MODEL),
        scratch_shapes=[pltpu.VMEM((2, 2, tm, D_MODEL), F32), pltpu.SemaphoreType.DMA((2,))],
    )
    return pl.pallas_call(
        functools.partial(_combine_kernel, tm=tm, final=final),
        grid_spec=grid_spec,
        out_shape=jax.ShapeDtypeStruct((n, D_MODEL), F32),
        compiler_params=_cparams(("arbitrary",)),
        name="moe_combine_ple",
    )(d1, d2, ys, gates, x1, p2, g_ple, wpg, wple, g_fin)


def _rope_tables(positions):
    half = ROPE_DIM // 2
    inv_freq = ROPE_THETA ** (-jnp.arange(0, ROPE_DIM, 2, dtype=F32) / ROPE_DIM)
    ang = positions.astype(F32).reshape(-1, 1) * inv_freq
    cos, sin = jnp.cos(ang), jnp.sin(ang)
    n = ang.shape[0]
    ones = jnp.ones((n, HEAD_DIM - ROPE_DIM), F32)
    zeros8 = jnp.zeros((n, half), F32)
    zeros48 = jnp.zeros((n, HEAD_DIM - ROPE_DIM), F32)
    c = jnp.concatenate([cos, cos, ones], axis=1)
    s1 = jnp.concatenate([-sin, zeros8, zeros48], axis=1)
    s2 = jnp.concatenate([zeros8, sin, zeros48], axis=1)
    rep = lambda t: jnp.tile(t, (1, LANES // HEAD_DIM))
    return rep(c), rep(s1), rep(s2)


def _pick(limit, n):
    t = min(limit, n)
    assert n % t == 0, (limit, n)
    return t


def kernel(x, p, positions, g_mix, w_in, lam_q1, lam_k1, lam_q2, lam_k2, g_subln, w_br_a, w_br_b, w_o, g_ffn, w_router_group, b_router_group, w_router_expert, b_router_expert, w_exp_gate, w_exp_up, w_exp_down, g_ple, w_ple, w_ple_gate, g_final):
    b, s, d = x.shape
    depth = w_in.shape[0]
    n = b * s
    assert d == D_MODEL and w_in.shape[2] == IN_WIDTH

    tm_in = _pick(512, n)
    tk = _pick(256, s)
    tq_sb = _pick(2 * tk, s)
    tq_da = _pick(4 * tk, s)
    tm_merge = _pick(512, n)
    tc = _pick(1024, n)
    burst = _pick(128, tc)
    tm_comb = _pick(256, n)
    assert tc % 1024 == 0 or tc == n

    n_assign = n * 2
    p_rows = ((n_assign + MOE_BLOCK - 1) // MOE_BLOCK) * MOE_BLOCK + N_EXPERTS * MOE_BLOCK
    n_blk = p_rows // MOE_BLOCK

    rope_c, rope_s1, rope_s2 = _rope_tables(positions)
    x2 = x.reshape(n, d)
    row2 = lambda v: v.reshape(1, -1)

    for i in range(depth):
        lam_init = 0.8 - 0.6 * math.exp(-0.3 * i)
        sbq, sbk, sbv, daq, dak, dav, ga, gb = _inproj(
            x2, row2(g_mix[i]), w_in[i].astype(BF16), rope_c, rope_s1, rope_s2, tm_in)
        r3 = lambda t: t.reshape(b, s, -1)
        ya = _sb_attention(r3(sbq), r3(sbk), r3(sbv), tq_sb, tk)
        yb = _da_attention(r3(daq), r3(dak), r3(dav), row2(lam_q1[i]), row2(lam_k1[i]),
                           row2(lam_q2[i]), row2(lam_k2[i]), g_subln[i].reshape(-1, 1),
                           tq_da, tk, lam_init)

        w_r = jnp.zeros((d, LANES), F32)
        w_r = w_r.at[:, :N_EXPERTS].set(w_router_expert[i])
        w_r = w_r.at[:, GROUP_LANE0:GROUP_LANE0 + N_GROUPS].set(w_router_group[i])
        wr_hi = w_r.astype(BF16)
        wr_lo = (w_r - wr_hi.astype(F32)).astype(BF16)
        b_r = jnp.zeros((1, LANES), F32)
        b_r = b_r.at[0, :N_EXPERTS].set(b_router_expert[i])
        b_r = b_r.at[0, GROUP_LANE0:GROUP_LANE0 + N_GROUPS].set(b_router_group[i])

        x1, h2, route, gates, counts = _merge(
            ya.reshape(n, -1), yb.reshape(n, -1), ga, gb, x2,
            w_br_a[i].astype(BF16), w_br_b[i].astype(BF16), w_o[i].astype(BF16),
            row2(g_ffn[i]), wr_hi, wr_lo, b_r, tm_merge)

        pst, blk_e, used = _plan(counts.reshape(LANES), n_blk)
        xs, d1, d2 = _scatter(pst, route[:, 0], route[:, 1], h2,
                              jnp.zeros((p_rows, d), F32), tc, burst)
        ys = _experts(blk_e, used, xs, w_exp_gate[i].astype(BF16), w_exp_up[i].astype(BF16),
                      w_exp_down[i].astype(BF16))
        x2 = _combine(d1, d2, ys, gates, x1, p[i].reshape(n, -1), row2(g_ple[i]),
                      w_ple_gate[i].astype(BF16), w_ple[i].astype(BF16), row2(g_final),
                      tm_comb, final=(i == depth - 1))

    return x2.reshape(b, s, d)
```

```python
import functools
import math

import numpy as np
import jax
import jax.numpy as jnp
from jax import lax
from jax.experimental import pallas as pl
from jax.experimental.pallas import tpu as pltpu

F32 = jnp.float32
BF16 = jnp.bfloat16
I32 = jnp.int32

D_MODEL = 1024
PLE_DIM = 256
RMS_EPS = 1e-6
CHUNK = 64
HEAD_DIM = 64
LANES = 128
SB_WIDTH = 512
DA_WIDTH = 512
ROPE_THETA = 500000.0
ROPE_DIM = 16
N_GROUPS = 4
EXPERTS_PER_GROUP = 8
N_EXPERTS = 32
EXPERT_HIDDEN = 256
MOE_BLOCK = 256
IN_WIDTH = 5120
OFF_SBQ, OFF_SBK, OFF_SBV, OFF_DAQ, OFF_DAK, OFF_DAV, OFF_GA, OFF_GB = (
    0, 512, 1024, 1536, 2048, 2560, 3072, 4096)
QK_SCALE = HEAD_DIM ** -0.5 * math.log2(math.e)
NEG_BIG = -1e30
UNDERFLOW_LOG2 = -160.0
GROUP_LANE0 = N_EXPERTS
VMEM_LIMIT = 56 * 1024 * 1024
ISSUE_UNROLL = 8


def _cparams(sem):
    return pltpu.CompilerParams(dimension_semantics=sem, vmem_limit_bytes=VMEM_LIMIT)


def _rms(x, g):
    return x * lax.rsqrt(jnp.mean(x * x, axis=-1, keepdims=True) + RMS_EPS) * g


def _inproj_kernel(x_ref, g_ref, w_ref, c_ref, s1_ref, s2_ref,
                   sbq_ref, sbk_ref, sbv_ref, daq_ref, dak_ref, dav_ref, ga_ref, gb_ref):
    h = _rms(x_ref[...], g_ref[...]).astype(BF16)

    def proj(off, width):
        return jnp.dot(h, w_ref[:, off:off + width], preferred_element_type=F32)

    sbq_ref[...] = (proj(OFF_SBQ, SB_WIDTH) * QK_SCALE).astype(BF16)
    sbk_ref[...] = proj(OFF_SBK, SB_WIDTH).astype(BF16)
    sbv_ref[...] = proj(OFF_SBV, SB_WIDTH).astype(BF16)

    c = c_ref[...]
    s1 = s1_ref[...]
    s2 = s2_ref[...]

    def rope_store(t, out_ref, scale):
        for j in range(t.shape[1] // LANES):
            tc = t[:, j * LANES:(j + 1) * LANES]
            r = tc * c + pltpu.roll(tc, LANES - 8, 1) * s1 + pltpu.roll(tc, 8, 1) * s2
            out_ref[:, j * LANES:(j + 1) * LANES] = (r * scale).astype(BF16)

    rope_store(proj(OFF_DAQ, DA_WIDTH), daq_ref, QK_SCALE)
    rope_store(proj(OFF_DAK, DA_WIDTH), dak_ref, 1.0)
    dav_ref[...] = proj(OFF_DAV, DA_WIDTH).astype(BF16)
    ga_ref[...] = jax.nn.sigmoid(proj(OFF_GA, D_MODEL)).astype(BF16)
    gb_ref[...] = jax.nn.sigmoid(proj(OFF_GB, D_MODEL)).astype(BF16)


def _inproj(x2, g, w_bf, rope_c, rope_s1, rope_s2, tm):
    n = x2.shape[0]
    row = lambda w: pl.BlockSpec((tm, w), lambda i: (i, 0))
    full = lambda a: pl.BlockSpec(a.shape, lambda i: (0,) * a.ndim)
    outs = [jax.ShapeDtypeStruct((n, w), BF16)
            for w in (SB_WIDTH,) * 3 + (DA_WIDTH,) * 3 + (D_MODEL,) * 2]
    return pl.pallas_call(
        _inproj_kernel,
        grid=(n // tm,),
        in_specs=[row(D_MODEL), full(g), full(w_bf), row(LANES), row(LANES), row(LANES)],
        out_specs=[row(s.shape[1]) for s in outs],
        out_shape=outs,
        compiler_params=_cparams(("parallel",)),
        name="inproj",
    )(x2, g, w_bf, rope_c, rope_s1, rope_s2)


def _kv_block(k_ref, vt_ref, kj, tk):
    start = pl.multiple_of(kj * tk, tk)
    return k_ref[0, pl.ds(start, tk), :], vt_ref[0, kj]


def _split_lane_halves(qt_ref, tk):
    row = lax.broadcasted_iota(I32, (LANES, 1), 0)
    first = row < HEAD_DIM
    out = []
    for s in range(qt_ref.shape[2] // tk):
        qt = qt_ref[0, :, s * tk:(s + 1) * tk]
        zero = jnp.zeros_like(qt)
        out.append((jnp.where(first, qt, zero), jnp.where(first, zero, qt)))
    return out, first


def _staged(items, scores, state, mid, fin, prefetch=None):
    mids = []
    for idx, (c, vt, diag) in enumerate(items):
        payload, state[c] = mid(c, scores[idx](), vt, diag, state[c])
        if prefetch is not None:
            prefetch(c)
        mids.append(payload)
    for (c, vt, diag), payload in zip(items, mids):
        state[c] = fin(c, vt, payload, diag, state[c])
    return state


def _sweep_key_blocks(qi, nsub, tk, k_ref, vt_ref, z_ref, qk, mid, fin, state, alive=None):
    state = list(state)
    nc = 2 * nsub
    items, scores = [], []
    for s_d in reversed(range(nsub)):
        k, vt = _kv_block(k_ref, vt_ref, qi * nsub + s_d, tk)
        for s in range(s_d, nsub):
            for h in range(2):
                c = 2 * s + h
                items.append((c, vt, s == s_d))
                z = qk(c, k)
                scores.append(lambda z=z: z)
    first_kj = jnp.maximum(qi * nsub - 1, 0)
    k_first, _ = _kv_block(k_ref, vt_ref, first_kj, tk)

    seen = set()

    def prefetch_first(c):
        if c not in seen:
            seen.add(c)
            z_ref[c] = qk(c, k_first)

    state = _staged(items, scores, state, mid, fin, prefetch_first)

    def body(i, st):
        kj = qi * nsub - 1 - i
        _, vt = _kv_block(k_ref, vt_ref, kj, tk)
        k_next, _ = _kv_block(k_ref, vt_ref, jnp.maximum(kj - 1, 0), tk)

        def prefetch(c):
            z_ref[c] = qk(c, k_next)

        loop_items = [(c, vt, False) for c in range(nc)]
        loop_scores = [lambda c=c: z_ref[c] for c in range(nc)]
        return tuple(_staged(loop_items, loop_scores, list(st), mid, fin, prefetch))

    n_loop = qi * nsub
    if alive is None:
        return lax.fori_loop(0, n_loop, body, tuple(state))

    def cond(carry):
        i, go, _ = carry
        return (i < n_loop) & go

    def step(carry):
        i, _, st = carry
        st = body(i, st)
        return i + 1, alive(st), st

    return lax.while_loop(cond, step, (jnp.int32(0), alive(state), tuple(state)))[2]


def _sb_kernel(qt_ref, k_ref, vt_ref, ut_ref, o_ref, acc_ref, z_ref, *, tk):
    qi = pl.program_id(2)
    nsub = qt_ref.shape[2] // tk
    qts, first = _split_lane_halves(qt_ref, tk)
    ut = ut_ref[...]
    acc_ref[...] = jnp.zeros_like(acc_ref)
    key = lax.broadcasted_iota(I32, (tk, tk), 0)
    qry = lax.broadcasted_iota(I32, (tk, tk), 1)

    def qk(c, k):
        return jnp.dot(k, qts[c // 2][c % 2], preferred_element_type=F32)

    def mid(c, z, vt, diag, carry):
        sp = jnp.log2(1.0 + jnp.exp2(-jnp.abs(z)))
        ls = jnp.minimum(z, 0.0) - sp
        lstay = ls - z
        if diag:
            lstay = jnp.where(key < qry, lstay, 0.0)
        later = jnp.dot(ut, lstay.astype(BF16), preferred_element_type=F32)
        return (ls, later, lstay[0:1, :]), carry

    def fin(c, vt, payload, diag, carry):
        ls, later, stay0 = payload
        w = jnp.exp2(ls + later + carry)
        if diag:
            w = jnp.where(key < qry, w, 0.0)
        acc_ref[c] += jnp.dot(vt, w.astype(BF16), preferred_element_type=F32)
        return carry + later[0:1, :] + stay0

    def alive(carries):
        top = functools.reduce(jnp.maximum, carries)
        return jnp.max(top) >= UNDERFLOW_LOG2

    zero = jnp.zeros((1, tk), F32)
    _sweep_key_blocks(qi, nsub, tk, k_ref, vt_ref, z_ref, qk, mid, fin, [zero] * (2 * nsub),
                      alive)
    for s in range(nsub):
        o_t = jnp.where(first, acc_ref[2 * s], acc_ref[2 * s + 1])
        o_ref[0, s * tk:(s + 1) * tk, :] = o_t.T.astype(o_ref.dtype)


def _attn_specs(s, tq, tk):
    qt_spec = pl.BlockSpec((1, LANES, tq), lambda bi, p, qi: (bi, p, qi))
    k_spec = pl.BlockSpec((1, s, LANES), lambda bi, p, qi: (bi, 0, p))
    vt_spec = pl.BlockSpec((1, s // tk, LANES, tk), lambda bi, p, qi: (bi, 0, p, 0))
    o_spec = pl.BlockSpec((1, tq, LANES), lambda bi, p, qi: (bi, qi, p))
    return qt_spec, k_spec, vt_spec, o_spec


def _transpose_q(q):
    return jnp.swapaxes(q, 1, 2)


def _transpose_v(v, tk):
    b, s, w = v.shape
    return jnp.swapaxes(v.reshape(b, s // tk, tk, w), 2, 3)


def _sb_attention(q, k, v, tq, tk):
    b, s, width = q.shape
    ut = jnp.asarray(np.triu(np.ones((tk, tk), np.float32), 1), BF16)
    qt_spec, k_spec, vt_spec, o_spec = _attn_specs(s, tq, tk)
    return pl.pallas_call(
        functools.partial(_sb_kernel, tk=tk),
        grid=(b, width // LANES, s // tq),
        in_specs=[qt_spec, k_spec, vt_spec, pl.BlockSpec((tk, tk), lambda bi, p, qi: (0, 0))],
        out_specs=o_spec,
        out_shape=jax.ShapeDtypeStruct((b, s, width), BF16),
        scratch_shapes=[pltpu.VMEM((2 * (tq // tk), LANES, tk), F32),
                        pltpu.VMEM((2 * (tq // tk), tk, tk), F32)],
        compiler_params=_cparams(("parallel", "parallel", "arbitrary")),
        name="sb_attn",
    )(_transpose_q(q), k, _transpose_v(v, tk), ut)


def _da_kernel(qt_ref, k_ref, vt_ref, lq1_ref, lk1_ref, lq2_ref, lk2_ref, g_ref, o_ref,
               acc_ref, z_ref, *, tk, lam_init):
    qi = pl.program_id(2)
    nsub = qt_ref.shape[2] // tk
    qts, _ = _split_lane_halves(qt_ref, tk)
    acc_ref[...] = jnp.zeros_like(acc_ref)
    key = lax.broadcasted_iota(I32, (tk, tk), 0)
    qry = lax.broadcasted_iota(I32, (tk, tk), 1)

    def qk(c, k):
        return jnp.dot(k, qts[c // 2][c % 2], preferred_element_type=F32)

    def mid(c, sc, vt, diag, st):
        mx, l = st
        if diag:
            sc = jnp.where((key // CHUNK) <= (qry // CHUNK), sc, NEG_BIG)
        m_new = jnp.maximum(mx, jnp.max(sc, axis=0, keepdims=True))
        p = jnp.exp2(sc - m_new)
        alpha = jnp.exp2(mx - m_new)
        l = alpha * l + jnp.sum(p, axis=0, keepdims=True)
        pv = jnp.dot(vt, p.astype(BF16), preferred_element_type=F32)
        return (alpha, pv), (m_new, l)

    def fin(c, vt, payload, diag, st):
        alpha, pv = payload
        acc_ref[c] = alpha * acc_ref[c] + pv
        return st

    init = (jnp.full((1, tk), NEG_BIG, F32), jnp.zeros((1, tk), F32))
    state = _sweep_key_blocks(qi, nsub, tk, k_ref, vt_ref, z_ref, qk, mid, fin,
                              [init] * (2 * nsub))

    lam = (jnp.exp(jnp.sum(lq1_ref[...] * lk1_ref[...], axis=-1, keepdims=True))
           - jnp.exp(jnp.sum(lq2_ref[...] * lk2_ref[...], axis=-1, keepdims=True))
           + lam_init)
    for s in range(nsub):
        o_t = (acc_ref[2 * s] / state[2 * s][1]
               - lam * (acc_ref[2 * s + 1] / state[2 * s + 1][1]))
        ms = jnp.mean(o_t * o_t, axis=0, keepdims=True)
        y_t = o_t * lax.rsqrt(ms + RMS_EPS) * g_ref[...] * (1.0 - lam_init)
        o_ref[0, s * tk:(s + 1) * tk, :] = y_t.T.astype(o_ref.dtype)


def _da_attention(q, k, v, lq1, lk1, lq2, lk2, g_col, tq, tk, lam_init):
    b, s, width = q.shape
    vec = lambda a: pl.BlockSpec(a.shape, lambda bi, h, qi: (0, 0))
    qt_spec, k_spec, vt_spec, o_spec = _attn_specs(s, tq, tk)
    return pl.pallas_call(
        functools.partial(_da_kernel, tk=tk, lam_init=lam_init),
        grid=(b, width // LANES, s // tq),
        in_specs=[qt_spec, k_spec, vt_spec, vec(lq1), vec(lk1), vec(lq2), vec(lk2), vec(g_col)],
        out_specs=o_spec,
        out_shape=jax.ShapeDtypeStruct((b, s, width), BF16),
        scratch_shapes=[pltpu.VMEM((2 * (tq // tk), LANES, tk), F32),
                        pltpu.VMEM((2 * (tq // tk), tk, tk), F32)],
        compiler_params=_cparams(("parallel", "parallel", "arbitrary")),
        name="da_attn",
    )(_transpose_q(q), k, _transpose_v(v, tk), lq1, lk1, lq2, lk2, g_col)


def _merge_kernel(ya_ref, yb_ref, ga_ref, gb_ref, x_ref, wa_ref, wb_ref, wo_ref, g_ref,
                  wrh_ref, wrl_ref, br_ref, tri_ref,
                  x1_ref, h2_ref, route_ref, gate_ref, cnt_ref, run_ref):
    @pl.when(pl.program_id(0) == 0)
    def _():
        run_ref[...] = jnp.zeros_like(run_ref)

    pa = jnp.dot(ya_ref[...], wa_ref[...], preferred_element_type=F32)
    pb = jnp.dot(yb_ref[...], wb_ref[...], preferred_element_type=F32)
    merged = ga_ref[...].astype(F32) * pa + gb_ref[...].astype(F32) * pb
    x1 = x_ref[...] + jnp.dot(merged.astype(BF16), wo_ref[...], preferred_element_type=F32)
    x1_ref[...] = x1
    h2 = _rms(x1, g_ref[...])
    h2_ref[...] = h2

    hh = h2.astype(BF16)
    hl = (h2 - hh.astype(F32)).astype(BF16)
    logits = (jnp.dot(hh, wrh_ref[...], preferred_element_type=F32)
              + jnp.dot(hh, wrl_ref[...], preferred_element_type=F32)
              + jnp.dot(hl, wrh_ref[...], preferred_element_type=F32)
              + br_ref[...])

    tm = logits.shape[0]
    lane = lax.broadcasted_iota(I32, (tm, LANES), 1)
    far = jnp.int32(4 * LANES)
    is_group = (lane >= GROUP_LANE0) & (lane < GROUP_LANE0 + N_GROUPS)
    gl = jnp.where(is_group, logits, -jnp.inf)
    gmax = jnp.max(gl, axis=-1, keepdims=True)
    gsel = jnp.min(jnp.where(gl == gmax, lane, far), axis=-1, keepdims=True) - GROUP_LANE0
    gsum = jnp.sum(jnp.where(is_group, jnp.exp(gl - gmax), 0.0), axis=-1, keepdims=True)
    g_p = 1.0 / gsum

    in_group = (lane >> 3) == gsel
    el = jnp.where(in_group, logits, -jnp.inf)
    m1 = jnp.max(el, axis=-1, keepdims=True)
    i1 = jnp.min(jnp.where(el == m1, lane, far), axis=-1, keepdims=True)
    el2 = jnp.where(lane == i1, -jnp.inf, el)
    m2 = jnp.max(el2, axis=-1, keepdims=True)
    i2 = jnp.min(jnp.where(el2 == m2, lane, far), axis=-1, keepdims=True)
    e2 = jnp.exp(m2 - m1)
    gate1 = g_p / (1.0 + e2)
    gate2 = gate1 * e2

    sel1 = lane == i1
    sel2 = lane == i2
    onehot = jnp.where(sel1 | sel2, 1.0, 0.0)
    before = jnp.dot(tri_ref[...], onehot.astype(BF16), preferred_element_type=F32)
    before = before + run_ref[...]
    r1 = jnp.sum(jnp.where(sel1, before, 0.0), axis=-1, keepdims=True).astype(I32)
    r2 = jnp.sum(jnp.where(sel2, before, 0.0), axis=-1, keepdims=True).astype(I32)
    run_ref[...] += jnp.sum(onehot, axis=0, keepdims=True)
    cnt_ref[...] = run_ref[...].astype(I32)

    code1 = (i1 << 16) | r1
    code2 = (i2 << 16) | r2
    route_ref[...] = jnp.where(lane == 0, code1, jnp.where(lane == 1, code2, 0))
    gate_ref[...] = jnp.where(lane == 0, gate1, jnp.where(lane == 1, gate2, 0.0))


def _merge(ya, yb, ga, gb, x2, wa, wb, wo, g_ffn, wr_hi, wr_lo, b_r, tm):
    n = x2.shape[0]
    tri = jnp.asarray(np.tril(np.ones((tm, tm), np.float32), -1), BF16)
    row = lambda w: pl.BlockSpec((tm, w), lambda i: (i, 0))
    full = lambda a: pl.BlockSpec(a.shape, lambda i: (0,) * a.ndim)
    return pl.pallas_call(
        _merge_kernel,
        grid=(n // tm,),
        in_specs=[row(SB_WIDTH), row(DA_WIDTH), row(D_MODEL), row(D_MODEL), row(D_MODEL),
                  full(wa), full(wb), full(wo), full(g_ffn), full(wr_hi), full(wr_lo),
                  full(b_r), full(tri)],
        out_specs=[row(D_MODEL), row(D_MODEL), row(LANES), row(LANES),
                   pl.BlockSpec((1, LANES), lambda i: (0, 0))],
        out_shape=[jax.ShapeDtypeStruct((n, D_MODEL), F32),
                   jax.ShapeDtypeStruct((n, D_MODEL), F32),
                   jax.ShapeDtypeStruct((n, LANES), I32),
                   jax.ShapeDtypeStruct((n, LANES), F32),
                   jax.ShapeDtypeStruct((1, LANES), I32)],
        scratch_shapes=[pltpu.VMEM((1, LANES), F32)],
        compiler_params=_cparams(("arbitrary",)),
        name="merge_route",
    )(ya, yb, ga, gb, x2, wa, wb, wo, g_ffn, wr_hi, wr_lo, b_r, tri)


def _plan_kernel(cnt_ref, pst_ref, blk_ref, pend_ref, *, n_blk):
    def seg(e, acc):
        pst_ref[e] = acc
        padded = ((cnt_ref[e] + (MOE_BLOCK - 1)) // MOE_BLOCK) * MOE_BLOCK
        pend_ref[e] = acc + padded
        return acc + padded

    lax.fori_loop(0, N_EXPERTS, seg, jnp.int32(0))

    def blk(b, e):
        e = lax.while_loop(
            lambda ee: (ee < N_EXPERTS - 1) & (pend_ref[ee] <= b * MOE_BLOCK),
            lambda ee: ee + 1, e)
        blk_ref[b] = e
        return e

    lax.fori_loop(0, n_blk, blk, jnp.int32(0))


def _plan(counts, n_blk):
    smem = pl.BlockSpec(memory_space=pltpu.SMEM)
    return pl.pallas_call(
        functools.partial(_plan_kernel, n_blk=n_blk),
        in_specs=[smem],
        out_specs=[smem, smem],
        out_shape=[jax.ShapeDtypeStruct((N_EXPERTS,), I32),
                   jax.ShapeDtypeStruct((n_blk,), I32)],
        scratch_shapes=[pltpu.SMEM((N_EXPERTS,), I32)],
        name="moe_plan",
    )(counts)


def _dest_kernel(pst_ref, code_ref, d_ref):
    code = code_ref[...]
    expert = code >> 16
    start = jnp.zeros_like(code)
    for e in range(N_EXPERTS):
        start = jnp.where(expert == e, pst_ref[e], start)
    d_ref[...] = start + (code & 0xFFFF)


def _dest(pst, codes):
    return pl.pallas_call(
        _dest_kernel,
        in_specs=[pl.BlockSpec(memory_space=pltpu.SMEM),
                  pl.BlockSpec(memory_space=pltpu.VMEM)],
        out_specs=pl.BlockSpec(memory_space=pltpu.VMEM),
        out_shape=jax.ShapeDtypeStruct(codes.shape, I32),
        name="moe_dest",
    )(pst, codes)


def _slotmap_kernel(d1_ref, d2_ref, tok_ref):
    i = pl.program_id(0)
    tc = d1_ref.shape[0]

    @pl.when(i == 0)
    def _():
        def clear(j, carry):
            tok_ref[j] = 0
            return carry

        lax.fori_loop(0, tok_ref.shape[0], clear, 0, unroll=ISSUE_UNROLL)

    def tok(t, carry):
        tok_ref[d1_ref[t]] = i * tc + t
        tok_ref[d2_ref[t]] = i * tc + t
        return carry

    lax.fori_loop(0, tc, tok, 0, unroll=ISSUE_UNROLL)


def _slotmap(d1, d2, p_rows, tc):
    n = d1.shape[0]
    smem_blk = pl.BlockSpec((tc,), lambda i: (i,), memory_space=pltpu.SMEM)
    return pl.pallas_call(
        _slotmap_kernel,
        grid=(n // tc,),
        in_specs=[smem_blk, smem_blk],
        out_specs=pl.BlockSpec(memory_space=pltpu.SMEM),
        out_shape=jax.ShapeDtypeStruct((p_rows,), I32),
        compiler_params=_cparams(("arbitrary",)),
        name="moe_slotmap",
    )(d1, d2)


def _expert_kernel(blk_ref, tok_ref, h_ref, wg_ref, wu_ref, wd_ref, ys_ref,
                   buf_ref, wgb_ref, wub_ref, wdb_ref, sem):
    b = pl.program_id(0)
    n_blk = pl.num_programs(0)
    slot = b % 2

    def issue_row(blk, slot, j):
        pltpu.make_async_copy(h_ref.at[pl.ds(tok_ref[blk * MOE_BLOCK + j], 1)],
                              buf_ref.at[slot, pl.ds(j, 1)], sem.at[slot]).start()

    def wait_slot(slot):
        pltpu.make_async_copy(h_ref.at[pl.ds(0, MOE_BLOCK)], buf_ref.at[slot],
                              sem.at[slot]).wait()

    @pl.when(b == 0)
    def _():
        def row(j, carry):
            issue_row(0, 0, j)
            return carry

        lax.fori_loop(0, MOE_BLOCK, row, 0, unroll=ISSUE_UNROLL)

    @pl.when((b == 0) | (blk_ref[b] != blk_ref[jnp.maximum(b - 1, 0)]))
    def _():
        wgb_ref[...] = wg_ref[0].astype(BF16)
        wub_ref[...] = wu_ref[0].astype(BF16)
        wdb_ref[...] = wd_ref[0].astype(BF16)

    wait_slot(slot)
    nxt = jnp.minimum(b + 1, n_blk - 1)
    for j in range(MOE_BLOCK):
        issue_row(nxt, 1 - slot, j)

    x = buf_ref[slot].astype(BF16)
    gate = jnp.dot(x, wgb_ref[...], preferred_element_type=F32)
    up = jnp.dot(x, wub_ref[...], preferred_element_type=F32)
    hid = (gate * jax.nn.sigmoid(gate) * up).astype(BF16)
    ys_ref[...] = jnp.dot(hid, wdb_ref[...], preferred_element_type=F32)

    @pl.when(b == n_blk - 1)
    def _():
        wait_slot(1 - slot)


def _experts(blk_e, slot_tok, h2, wg, wu, wd):
    p_rows = slot_tok.shape[0]
    n_blk = p_rows // MOE_BLOCK
    grid_spec = pltpu.PrefetchScalarGridSpec(
        num_scalar_prefetch=2,
        grid=(n_blk,),
        in_specs=[
            pl.BlockSpec(memory_space=pl.ANY),
            pl.BlockSpec((1, D_MODEL, EXPERT_HIDDEN), lambda b, blk, tok: (blk[b], 0, 0)),
            pl.BlockSpec((1, D_MODEL, EXPERT_HIDDEN), lambda b, blk, tok: (blk[b], 0, 0)),
            pl.BlockSpec((1, EXPERT_HIDDEN, D_MODEL), lambda b, blk, tok: (blk[b], 0, 0)),
        ],
        out_specs=pl.BlockSpec((MOE_BLOCK, D_MODEL), lambda b, blk, tok: (b, 0)),
        scratch_shapes=[pltpu.VMEM((2, MOE_BLOCK, D_MODEL), F32),
                        pltpu.VMEM((D_MODEL, EXPERT_HIDDEN), BF16),
                        pltpu.VMEM((D_MODEL, EXPERT_HIDDEN), BF16),
                        pltpu.VMEM((EXPERT_HIDDEN, D_MODEL), BF16),
                        pltpu.SemaphoreType.DMA((2,))],
    )
    return pl.pallas_call(
        _expert_kernel,
        grid_spec=grid_spec,
        out_shape=jax.ShapeDtypeStruct((p_rows, D_MODEL), F32),
        compiler_params=_cparams(("arbitrary",)),
        name="moe_experts",
    )(blk_e, slot_tok, h2, wg, wu, wd)


def _combine_kernel(d1_ref, d2_ref, ys_ref, gate_ref, x1_ref, p_ref, gple_ref, wpg_ref,
                    wple_ref, gfin_ref, o_ref, buf_ref, sem, *, tm, final):
    i = pl.program_id(0)
    n_steps = pl.num_programs(0)

    def issue_row(step, slot, j):
        t = step * tm + j
        pltpu.make_async_copy(ys_ref.at[pl.ds(d1_ref[t], 1)],
                              buf_ref.at[slot, 0, pl.ds(j, 1)], sem.at[slot]).start()
        pltpu.make_async_copy(ys_ref.at[pl.ds(d2_ref[t], 1)],
                              buf_ref.at[slot, 1, pl.ds(j, 1)], sem.at[slot]).start()

    def wait_slot(slot):
        for half in range(2):
            pltpu.make_async_copy(ys_ref.at[pl.ds(0, tm)], buf_ref.at[slot, half],
                                  sem.at[slot]).wait()

    slot = i % 2

    @pl.when(i == 0)
    def _():
        def tok(j, carry):
            issue_row(0, 0, j)
            return carry

        lax.fori_loop(0, tm, tok, 0, unroll=ISSUE_UNROLL)

    wait_slot(slot)
    nxt = jnp.minimum(i + 1, n_steps - 1)
    for j in range(tm):
        issue_row(nxt, 1 - slot, j)

    gates = gate_ref[...]
    x2 = (x1_ref[...] + gates[:, 0:1] * buf_ref[slot, 0] + gates[:, 1:2] * buf_ref[slot, 1])
    hn = _rms(x2, gple_ref[...]).astype(BF16)
    pg = jax.nn.sigmoid(jnp.dot(hn, wpg_ref[...], preferred_element_type=F32))
    pe = jnp.dot(p_ref[...].astype(BF16), wple_ref[...], preferred_element_type=F32)
    x3 = x2 + pe * pg
    if final:
        x3 = _rms(x3, gfin_ref[...])
    o_ref[...] = x3

    @pl.when(i == n_steps - 1)
    def _():
        wait_slot(1 - slot)


def _combine(d1, d2, ys, gates, x1, p2, g_ple, wpg, wple, g_fin, tm, final):
    n = x1.shape[0]
    row = lambda w: pl.BlockSpec((tm, w), lambda i, a, b: (i, 0))
    full = lambda arr: pl.BlockSpec(arr.shape, lambda i, a, b: (0,) * arr.ndim)
    grid_spec = pltpu.PrefetchScalarGridSpec(
        num_scalar_prefetch=2,
        grid=(n // tm,),
        in_specs=[pl.BlockSpec(memory_space=pl.ANY), row(LANES), row(D_MODEL), row(PLE_DIM),
                  full(g_ple), full(wpg), full(wple), full(g_fin)],
        out_specs=row(D_MODEL),
        scratch_shapes=[pltpu.VMEM((2, 2, tm, D_MODEL), F32), pltpu.SemaphoreType.DMA((2,))],
    )
    return pl.pallas_call(
        functools.partial(_combine_kernel, tm=tm, final=final),
        grid_spec=grid_spec,
        out_shape=jax.ShapeDtypeStruct((n, D_MODEL), F32),
        compiler_params=_cparams(("arbitrary",)),
        name="moe_combine_ple",
    )(d1, d2, ys, gates, x1, p2, g_ple, wpg, wple, g_fin)


def _rope_tables(positions):
    half = ROPE_DIM // 2
    inv_freq = ROPE_THETA ** (-jnp.arange(0, ROPE_DIM, 2, dtype=F32) / ROPE_DIM)
    ang = positions.astype(F32).reshape(-1, 1) * inv_freq
    cos, sin = jnp.cos(ang), jnp.sin(ang)
    n = ang.shape[0]
    ones = jnp.ones((n, HEAD_DIM - ROPE_DIM), F32)
    zeros8 = jnp.zeros((n, half), F32)
    zeros48 = jnp.zeros((n, HEAD_DIM - ROPE_DIM), F32)
    c = jnp.concatenate([cos, cos, ones], axis=1)
    s1 = jnp.concatenate([-sin, zeros8, zeros48], axis=1)
    s2 = jnp.concatenate([zeros8, sin, zeros48], axis=1)
    rep = lambda t: jnp.tile(t, (1, LANES // HEAD_DIM))
    return rep(c), rep(s1), rep(s2)


def _pick(limit, n):
    t = min(limit, n)
    assert n % t == 0, (limit, n)
    return t


def kernel(x, p, positions, g_mix, w_in, lam_q1, lam_k1, lam_q2, lam_k2, g_subln, w_br_a, w_br_b, w_o, g_ffn, w_router_group, b_router_group, w_router_expert, b_router_expert, w_exp_gate, w_exp_up, w_exp_down, g_ple, w_ple, w_ple_gate, g_final):
    b, s, d = x.shape
    depth = w_in.shape[0]
    n = b * s
    assert d == D_MODEL and w_in.shape[2] == IN_WIDTH

    tm_in = _pick(512, n)
    tk = _pick(256, s)
    tq_sb = _pick(2 * tk, s)
    tq_da = _pick(4 * tk, s)
    tm_merge = _pick(512, n)
    tc = _pick(4096, n)
    tm_comb = _pick(256, n)
    assert tc % 1024 == 0 or tc == n

    n_assign = n * 2
    p_rows = ((n_assign + MOE_BLOCK - 1) // MOE_BLOCK) * MOE_BLOCK + N_EXPERTS * MOE_BLOCK
    n_blk = p_rows // MOE_BLOCK

    rope_c, rope_s1, rope_s2 = _rope_tables(positions)
    x2 = x.reshape(n, d)
    row2 = lambda v: v.reshape(1, -1)

    for i in range(depth):
        lam_init = 0.8 - 0.6 * math.exp(-0.3 * i)
        sbq, sbk, sbv, daq, dak, dav, ga, gb = _inproj(
            x2, row2(g_mix[i]), w_in[i].astype(BF16), rope_c, rope_s1, rope_s2, tm_in)
        r3 = lambda t: t.reshape(b, s, -1)
        ya = _sb_attention(r3(sbq), r3(sbk), r3(sbv), tq_sb, tk)
        yb = _da_attention(r3(daq), r3(dak), r3(dav), row2(lam_q1[i]), row2(lam_k1[i]),
                           row2(lam_q2[i]), row2(lam_k2[i]), g_subln[i].reshape(-1, 1),
                           tq_da, tk, lam_init)

        w_r = jnp.zeros((d, LANES), F32)
        w_r = w_r.at[:, :N_EXPERTS].set(w_router_expert[i])
        w_r = w_r.at[:, GROUP_LANE0:GROUP_LANE0 + N_GROUPS].set(w_router_group[i])
        wr_hi = w_r.astype(BF16)
        wr_lo = (w_r - wr_hi.astype(F32)).astype(BF16)
        b_r = jnp.zeros((1, LANES), F32)
        b_r = b_r.at[0, :N_EXPERTS].set(b_router_expert[i])
        b_r = b_r.at[0, GROUP_LANE0:GROUP_LANE0 + N_GROUPS].set(b_router_group[i])

        x1, h2, route, gates, counts = _merge(
            ya.reshape(n, -1), yb.reshape(n, -1), ga, gb, x2,
            w_br_a[i].astype(BF16), w_br_b[i].astype(BF16), w_o[i].astype(BF16),
            row2(g_ffn[i]), wr_hi, wr_lo, b_r, tm_merge)

        pst, blk_e = _plan(counts.reshape(LANES), n_blk)
        codes = jnp.stack([route[:, 0], route[:, 1]]).reshape(2, n // LANES, LANES)
        dest = _dest(pst, codes).reshape(2, n)
        d1, d2 = dest[0], dest[1]
        slot_tok = _slotmap(d1, d2, p_rows, tc)
        ys = _experts(blk_e, slot_tok, h2, w_exp_gate[i], w_exp_up[i], w_exp_down[i])
        x2 = _combine(d1, d2, ys, gates, x1, p[i].reshape(n, -1), row2(g_ple[i]),
                      w_ple_gate[i].astype(BF16), w_ple[i].astype(BF16), row2(g_final),
                      tm_comb, final=(i == depth - 1))

    return x2.reshape(b, s, d)
```

```python
import functools
import math

import numpy as np
import jax
import jax.numpy as jnp
from jax import lax
from jax.experimental import pallas as pl
from jax.experimental.pallas import tpu as pltpu

F32 = jnp.float32
BF16 = jnp.bfloat16
I32 = jnp.int32

D_MODEL = 1024
PLE_DIM = 256
RMS_EPS = 1e-6
CHUNK = 64
HEAD_DIM = 64
LANES = 128
SB_WIDTH = 512
DA_WIDTH = 512
ROPE_THETA = 500000.0
ROPE_DIM = 16
N_GROUPS = 4
EXPERTS_PER_GROUP = 8
N_EXPERTS = 32
EXPERT_HIDDEN = 256
MOE_BLOCK = 256
IN_WIDTH = 5120
OFF_SBQ, OFF_SBK, OFF_SBV, OFF_DAQ, OFF_DAK, OFF_DAV, OFF_GA, OFF_GB = (
    0, 512, 1024, 1536, 2048, 2560, 3072, 4096)
QK_SCALE = HEAD_DIM ** -0.5 * math.log2(math.e)
NEG_BIG = -1e30
UNDERFLOW_LOG2 = -160.0
GROUP_LANE0 = N_EXPERTS
VMEM_LIMIT = 56 * 1024 * 1024
ISSUE_UNROLL = 8


def _cparams(sem):
    return pltpu.CompilerParams(dimension_semantics=sem, vmem_limit_bytes=VMEM_LIMIT)


def _rms(x, g):
    return x * lax.rsqrt(jnp.mean(x * x, axis=-1, keepdims=True) + RMS_EPS) * g


def _inproj_kernel(x_ref, g_ref, w_ref, c_ref, s1_ref, s2_ref,
                   sbq_ref, sbk_ref, sbv_ref, daq_ref, dak_ref, dav_ref, ga_ref, gb_ref):
    h = _rms(x_ref[...], g_ref[...]).astype(BF16)

    def proj(off, width):
        return jnp.dot(h, w_ref[:, off:off + width], preferred_element_type=F32)

    sbq_ref[...] = (proj(OFF_SBQ, SB_WIDTH) * QK_SCALE).astype(BF16)
    sbk_ref[...] = proj(OFF_SBK, SB_WIDTH).astype(BF16)
    sbv_ref[...] = proj(OFF_SBV, SB_WIDTH).astype(BF16)

    c = c_ref[...]
    s1 = s1_ref[...]
    s2 = s2_ref[...]

    def rope_store(t, out_ref, scale):
        for j in range(t.shape[1] // LANES):
            tc = t[:, j * LANES:(j + 1) * LANES]
            r = tc * c + pltpu.roll(tc, LANES - 8, 1) * s1 + pltpu.roll(tc, 8, 1) * s2
            out_ref[:, j * LANES:(j + 1) * LANES] = (r * scale).astype(BF16)

    rope_store(proj(OFF_DAQ, DA_WIDTH), daq_ref, QK_SCALE)
    rope_store(proj(OFF_DAK, DA_WIDTH), dak_ref, 1.0)
    dav_ref[...] = proj(OFF_DAV, DA_WIDTH).astype(BF16)
    ga_ref[...] = jax.nn.sigmoid(proj(OFF_GA, D_MODEL)).astype(BF16)
    gb_ref[...] = jax.nn.sigmoid(proj(OFF_GB, D_MODEL)).astype(BF16)


def _inproj(x2, g, w_bf, rope_c, rope_s1, rope_s2, tm):
    n = x2.shape[0]
    row = lambda w: pl.BlockSpec((tm, w), lambda i: (i, 0))
    full = lambda a: pl.BlockSpec(a.shape, lambda i: (0,) * a.ndim)
    outs = [jax.ShapeDtypeStruct((n, w), BF16)
            for w in (SB_WIDTH,) * 3 + (DA_WIDTH,) * 3 + (D_MODEL,) * 2]
    return pl.pallas_call(
        _inproj_kernel,
        grid=(n // tm,),
        in_specs=[row(D_MODEL), full(g), full(w_bf), row(LANES), row(LANES), row(LANES)],
        out_specs=[row(s.shape[1]) for s in outs],
        out_shape=outs,
        compiler_params=_cparams(("parallel",)),
        name="inproj",
    )(x2, g, w_bf, rope_c, rope_s1, rope_s2)


def _kv_block(k_ref, vt_ref, kj, tk):
    start = pl.multiple_of(kj * tk, tk)
    return k_ref[0, pl.ds(start, tk), :], vt_ref[0, kj]


def _split_lane_halves(qt_ref, tk):
    row = lax.broadcasted_iota(I32, (LANES, 1), 0)
    first = row < HEAD_DIM
    out = []
    for s in range(qt_ref.shape[2] // tk):
        qt = qt_ref[0, :, s * tk:(s + 1) * tk]
        zero = jnp.zeros_like(qt)
        out.append((jnp.where(first, qt, zero), jnp.where(first, zero, qt)))
    return out, first


def _staged(items, scores, state, mid, fin, prefetch=None):
    mids = []
    for idx, (c, vt, diag) in enumerate(items):
        payload, state[c] = mid(c, scores[idx](), vt, diag, state[c])
        if prefetch is not None:
            prefetch(c)
        mids.append(payload)
    for (c, vt, diag), payload in zip(items, mids):
        state[c] = fin(c, vt, payload, diag, state[c])
    return state


def _sweep_key_blocks(qi, nsub, tk, k_ref, vt_ref, z_ref, qk, mid, fin, state, alive=None):
    state = list(state)
    nc = 2 * nsub
    items, scores = [], []
    for s_d in reversed(range(nsub)):
        k, vt = _kv_block(k_ref, vt_ref, qi * nsub + s_d, tk)
        for s in range(s_d, nsub):
            for h in range(2):
                c = 2 * s + h
                items.append((c, vt, s == s_d))
                z = qk(c, k)
                scores.append(lambda z=z: z)
    first_kj = jnp.maximum(qi * nsub - 1, 0)
    k_first, _ = _kv_block(k_ref, vt_ref, first_kj, tk)

    seen = set()

    def prefetch_first(c):
        if c not in seen:
            seen.add(c)
            z_ref[c] = qk(c, k_first)

    state = _staged(items, scores, state, mid, fin, prefetch_first)

    def body(i, st):
        kj = qi * nsub - 1 - i
        _, vt = _kv_block(k_ref, vt_ref, kj, tk)
        k_next, _ = _kv_block(k_ref, vt_ref, jnp.maximum(kj - 1, 0), tk)

        def prefetch(c):
            z_ref[c] = qk(c, k_next)

        loop_items = [(c, vt, False) for c in range(nc)]
        loop_scores = [lambda c=c: z_ref[c] for c in range(nc)]
        return tuple(_staged(loop_items, loop_scores, list(st), mid, fin, prefetch))

    n_loop = qi * nsub
    if alive is None:
        return lax.fori_loop(0, n_loop, body, tuple(state))

    def cond(carry):
        i, go, _ = carry
        return (i < n_loop) & go

    def step(carry):
        i, _, st = carry
        st = body(i, st)
        return i + 1, alive(st), st

    return lax.while_loop(cond, step, (jnp.int32(0), alive(state), tuple(state)))[2]


def _sb_kernel(qt_ref, k_ref, vt_ref, ut_ref, o_ref, acc_ref, z_ref, *, tk):
    qi = pl.program_id(2)
    nsub = qt_ref.shape[2] // tk
    qts, first = _split_lane_halves(qt_ref, tk)
    ut = ut_ref[...]
    acc_ref[...] = jnp.zeros_like(acc_ref)
    key = lax.broadcasted_iota(I32, (tk, tk), 0)
    qry = lax.broadcasted_iota(I32, (tk, tk), 1)

    def qk(c, k):
        return jnp.dot(k, qts[c // 2][c % 2], preferred_element_type=F32)

    def mid(c, z, vt, diag, carry):
        sp = jnp.log2(1.0 + jnp.exp2(-jnp.abs(z)))
        ls = jnp.minimum(z, 0.0) - sp
        lstay = ls - z
        if diag:
            lstay = jnp.where(key < qry, lstay, 0.0)
        later = jnp.dot(ut, lstay.astype(BF16), preferred_element_type=F32)
        return (ls, later, lstay[0:1, :]), carry

    def fin(c, vt, payload, diag, carry):
        ls, later, stay0 = payload
        w = jnp.exp2(ls + later + carry)
        if diag:
            w = jnp.where(key < qry, w, 0.0)
        acc_ref[c] += jnp.dot(vt, w.astype(BF16), preferred_element_type=F32)
        return carry + later[0:1, :] + stay0

    def alive(carries):
        top = functools.reduce(jnp.maximum, carries)
        return jnp.max(top) >= UNDERFLOW_LOG2

    zero = jnp.zeros((1, tk), F32)
    _sweep_key_blocks(qi, nsub, tk, k_ref, vt_ref, z_ref, qk, mid, fin, [zero] * (2 * nsub),
                      alive)
    for s in range(nsub):
        o_t = jnp.where(first, acc_ref[2 * s], acc_ref[2 * s + 1])
        o_ref[0, s * tk:(s + 1) * tk, :] = o_t.T.astype(o_ref.dtype)


def _attn_specs(s, tq, tk):
    qt_spec = pl.BlockSpec((1, LANES, tq), lambda bi, p, qi: (bi, p, qi))
    k_spec = pl.BlockSpec((1, s, LANES), lambda bi, p, qi: (bi, 0, p))
    vt_spec = pl.BlockSpec((1, s // tk, LANES, tk), lambda bi, p, qi: (bi, 0, p, 0))
    o_spec = pl.BlockSpec((1, tq, LANES), lambda bi, p, qi: (bi, qi, p))
    return qt_spec, k_spec, vt_spec, o_spec


def _transpose_q(q):
    return jnp.swapaxes(q, 1, 2)


def _transpose_v(v, tk):
    b, s, w = v.shape
    return jnp.swapaxes(v.reshape(b, s // tk, tk, w), 2, 3)


def _sb_attention(q, k, v, tq, tk):
    b, s, width = q.shape
    ut = jnp.asarray(np.triu(np.ones((tk, tk), np.float32), 1), BF16)
    qt_spec, k_spec, vt_spec, o_spec = _attn_specs(s, tq, tk)
    return pl.pallas_call(
        functools.partial(_sb_kernel, tk=tk),
        grid=(b, width // LANES, s // tq),
        in_specs=[qt_spec, k_spec, vt_spec, pl.BlockSpec((tk, tk), lambda bi, p, qi: (0, 0))],
        out_specs=o_spec,
        out_shape=jax.ShapeDtypeStruct((b, s, width), BF16),
        scratch_shapes=[pltpu.VMEM((2 * (tq // tk), LANES, tk), F32),
                        pltpu.VMEM((2 * (tq // tk), tk, tk), F32)],
        compiler_params=_cparams(("parallel", "parallel", "arbitrary")),
        name="sb_attn",
    )(_transpose_q(q), k, _transpose_v(v, tk), ut)


def _da_kernel(qt_ref, k_ref, vt_ref, lq1_ref, lk1_ref, lq2_ref, lk2_ref, g_ref, o_ref,
               acc_ref, z_ref, *, tk, lam_init):
    qi = pl.program_id(2)
    nsub = qt_ref.shape[2] // tk
    qts, _ = _split_lane_halves(qt_ref, tk)
    acc_ref[...] = jnp.zeros_like(acc_ref)
    key = lax.broadcasted_iota(I32, (tk, tk), 0)
    qry = lax.broadcasted_iota(I32, (tk, tk), 1)

    def qk(c, k):
        return jnp.dot(k, qts[c // 2][c % 2], preferred_element_type=F32)

    def mid(c, sc, vt, diag, st):
        mx, l = st
        if diag:
            sc = jnp.where((key // CHUNK) <= (qry // CHUNK), sc, NEG_BIG)
        m_new = jnp.maximum(mx, jnp.max(sc, axis=0, keepdims=True))
        p = jnp.exp2(sc - m_new)
        alpha = jnp.exp2(mx - m_new)
        l = alpha * l + jnp.sum(p, axis=0, keepdims=True)
        pv = jnp.dot(vt, p.astype(BF16), preferred_element_type=F32)
        return (alpha, pv), (m_new, l)

    def fin(c, vt, payload, diag, st):
        alpha, pv = payload
        acc_ref[c] = alpha * acc_ref[c] + pv
        return st

    init = (jnp.full((1, tk), NEG_BIG, F32), jnp.zeros((1, tk), F32))
    state = _sweep_key_blocks(qi, nsub, tk, k_ref, vt_ref, z_ref, qk, mid, fin,
                              [init] * (2 * nsub))

    lam = (jnp.exp(jnp.sum(lq1_ref[...] * lk1_ref[...], axis=-1, keepdims=True))
           - jnp.exp(jnp.sum(lq2_ref[...] * lk2_ref[...], axis=-1, keepdims=True))
           + lam_init)
    for s in range(nsub):
        o_t = (acc_ref[2 * s] / state[2 * s][1]
               - lam * (acc_ref[2 * s + 1] / state[2 * s + 1][1]))
        ms = jnp.mean(o_t * o_t, axis=0, keepdims=True)
        y_t = o_t * lax.rsqrt(ms + RMS_EPS) * g_ref[...] * (1.0 - lam_init)
        o_ref[0, s * tk:(s + 1) * tk, :] = y_t.T.astype(o_ref.dtype)


def _da_attention(q, k, v, lq1, lk1, lq2, lk2, g_col, tq, tk, lam_init):
    b, s, width = q.shape
    vec = lambda a: pl.BlockSpec(a.shape, lambda bi, h, qi: (0, 0))
    qt_spec, k_spec, vt_spec, o_spec = _attn_specs(s, tq, tk)
    return pl.pallas_call(
        functools.partial(_da_kernel, tk=tk, lam_init=lam_init),
        grid=(b, width // LANES, s // tq),
        in_specs=[qt_spec, k_spec, vt_spec, vec(lq1), vec(lk1), vec(lq2), vec(lk2), vec(g_col)],
        out_specs=o_spec,
        out_shape=jax.ShapeDtypeStruct((b, s, width), BF16),
        scratch_shapes=[pltpu.VMEM((2 * (tq // tk), LANES, tk), F32),
                        pltpu.VMEM((2 * (tq // tk), tk, tk), F32)],
        compiler_params=_cparams(("parallel", "parallel", "arbitrary")),
        name="da_attn",
    )(_transpose_q(q), k, _transpose_v(v, tk), lq1, lk1, lq2, lk2, g_col)


def _merge_kernel(ya_ref, yb_ref, ga_ref, gb_ref, x_ref, wa_ref, wb_ref, wo_ref, g_ref,
                  wrh_ref, wrl_ref, br_ref, tri_ref,
                  x1_ref, h2_ref, route_ref, gate_ref, cnt_ref, run_ref):
    @pl.when(pl.program_id(0) == 0)
    def _():
        run_ref[...] = jnp.zeros_like(run_ref)

    pa = jnp.dot(ya_ref[...], wa_ref[...], preferred_element_type=F32)
    pb = jnp.dot(yb_ref[...], wb_ref[...], preferred_element_type=F32)
    merged = ga_ref[...].astype(F32) * pa + gb_ref[...].astype(F32) * pb
    x1 = x_ref[...] + jnp.dot(merged.astype(BF16), wo_ref[...], preferred_element_type=F32)
    x1_ref[...] = x1
    h2 = _rms(x1, g_ref[...])
    h2_ref[...] = h2

    hh = h2.astype(BF16)
    hl = (h2 - hh.astype(F32)).astype(BF16)
    logits = (jnp.dot(hh, wrh_ref[...], preferred_element_type=F32)
              + jnp.dot(hh, wrl_ref[...], preferred_element_type=F32)
              + jnp.dot(hl, wrh_ref[...], preferred_element_type=F32)
              + br_ref[...])

    tm = logits.shape[0]
    lane = lax.broadcasted_iota(I32, (tm, LANES), 1)
    far = jnp.int32(4 * LANES)
    is_group = (lane >= GROUP_LANE0) & (lane < GROUP_LANE0 + N_GROUPS)
    gl = jnp.where(is_group, logits, -jnp.inf)
    gmax = jnp.max(gl, axis=-1, keepdims=True)
    gsel = jnp.min(jnp.where(gl == gmax, lane, far), axis=-1, keepdims=True) - GROUP_LANE0
    gsum = jnp.sum(jnp.where(is_group, jnp.exp(gl - gmax), 0.0), axis=-1, keepdims=True)
    g_p = 1.0 / gsum

    in_group = (lane >> 3) == gsel
    el = jnp.where(in_group, logits, -jnp.inf)
    m1 = jnp.max(el, axis=-1, keepdims=True)
    i1 = jnp.min(jnp.where(el == m1, lane, far), axis=-1, keepdims=True)
    el2 = jnp.where(lane == i1, -jnp.inf, el)
    m2 = jnp.max(el2, axis=-1, keepdims=True)
    i2 = jnp.min(jnp.where(el2 == m2, lane, far), axis=-1, keepdims=True)
    e2 = jnp.exp(m2 - m1)
    gate1 = g_p / (1.0 + e2)
    gate2 = gate1 * e2

    sel1 = lane == i1
    sel2 = lane == i2
    onehot = jnp.where(sel1 | sel2, 1.0, 0.0)
    before = jnp.dot(tri_ref[...], onehot.astype(BF16), preferred_element_type=F32)
    before = before + run_ref[...]
    r1 = jnp.sum(jnp.where(sel1, before, 0.0), axis=-1, keepdims=True).astype(I32)
    r2 = jnp.sum(jnp.where(sel2, before, 0.0), axis=-1, keepdims=True).astype(I32)
    run_ref[...] += jnp.sum(onehot, axis=0, keepdims=True)
    cnt_ref[...] = run_ref[...].astype(I32)

    code1 = (i1 << 16) | r1
    code2 = (i2 << 16) | r2
    route = jnp.where(lane == 0, code1, jnp.where(lane == 1, code2, 0))
    route_ref[...] = route.T[0:8, :]
    gate_ref[...] = jnp.where(lane == 0, gate1, jnp.where(lane == 1, gate2, 0.0))


def _merge(ya, yb, ga, gb, x2, wa, wb, wo, g_ffn, wr_hi, wr_lo, b_r, tm):
    n = x2.shape[0]
    tri = jnp.asarray(np.tril(np.ones((tm, tm), np.float32), -1), BF16)
    row = lambda w: pl.BlockSpec((tm, w), lambda i: (i, 0))
    full = lambda a: pl.BlockSpec(a.shape, lambda i: (0,) * a.ndim)
    return pl.pallas_call(
        _merge_kernel,
        grid=(n // tm,),
        in_specs=[row(SB_WIDTH), row(DA_WIDTH), row(D_MODEL), row(D_MODEL), row(D_MODEL),
                  full(wa), full(wb), full(wo), full(g_ffn), full(wr_hi), full(wr_lo),
                  full(b_r), full(tri)],
        out_specs=[row(D_MODEL), row(D_MODEL), pl.BlockSpec((8, tm), lambda i: (0, i)),
                   row(LANES), pl.BlockSpec((1, LANES), lambda i: (0, 0))],
        out_shape=[jax.ShapeDtypeStruct((n, D_MODEL), F32),
                   jax.ShapeDtypeStruct((n, D_MODEL), F32),
                   jax.ShapeDtypeStruct((8, n), I32),
                   jax.ShapeDtypeStruct((n, LANES), F32),
                   jax.ShapeDtypeStruct((1, LANES), I32)],
        scratch_shapes=[pltpu.VMEM((1, LANES), F32)],
        compiler_params=_cparams(("arbitrary",)),
        name="merge_route",
    )(ya, yb, ga, gb, x2, wa, wb, wo, g_ffn, wr_hi, wr_lo, b_r, tri)


def _plan_kernel(cnt_ref, pst_ref, blk_ref, used_ref, pend_ref, *, n_blk):
    def seg(e, acc):
        pst_ref[e] = acc
        padded = ((cnt_ref[e] + (MOE_BLOCK - 1)) // MOE_BLOCK) * MOE_BLOCK
        pend_ref[e] = acc + padded
        return acc + padded

    total = lax.fori_loop(0, N_EXPERTS, seg, jnp.int32(0))
    used_ref[0] = total // MOE_BLOCK

    def blk(b, e):
        e = lax.while_loop(
            lambda ee: (ee < N_EXPERTS - 1) & (pend_ref[ee] <= b * MOE_BLOCK),
            lambda ee: ee + 1, e)
        blk_ref[b] = e
        return e

    lax.fori_loop(0, n_blk, blk, jnp.int32(0))


def _plan(counts, n_blk):
    smem = pl.BlockSpec(memory_space=pltpu.SMEM)
    return pl.pallas_call(
        functools.partial(_plan_kernel, n_blk=n_blk),
        in_specs=[smem],
        out_specs=[smem, smem, smem],
        out_shape=[jax.ShapeDtypeStruct((N_EXPERTS,), I32),
                   jax.ShapeDtypeStruct((n_blk,), I32),
                   jax.ShapeDtypeStruct((1,), I32)],
        scratch_shapes=[pltpu.SMEM((N_EXPERTS,), I32)],
        name="moe_plan",
    )(counts)


def _dest_kernel(pst_ref, code_ref, d_ref):
    code = code_ref[...]
    expert = code >> 16
    start = jnp.zeros_like(code)
    for e in range(N_EXPERTS):
        start = jnp.where(expert == e, pst_ref[e], start)
    d_ref[...] = start + (code & 0xFFFF)


def _dest(pst, codes):
    return pl.pallas_call(
        _dest_kernel,
        in_specs=[pl.BlockSpec(memory_space=pltpu.SMEM),
                  pl.BlockSpec(memory_space=pltpu.VMEM)],
        out_specs=pl.BlockSpec(memory_space=pltpu.VMEM),
        out_shape=jax.ShapeDtypeStruct(codes.shape, I32),
        name="moe_dest",
    )(pst, codes)


def _scatter_kernel(d1_ref, d2_ref, h_ref, xs_in_ref, xs_ref, sem, *, burst):
    del xs_in_ref
    tc = h_ref.shape[0]

    def wait_burst():
        pltpu.make_async_copy(h_ref.at[pl.ds(0, 2 * burst)], xs_ref.at[pl.ds(0, 2 * burst)],
                              sem).wait()

    def do_burst(c, carry):
        def tok(j, carry2):
            t = c * burst + j
            src = h_ref.at[pl.ds(t, 1)]
            pltpu.make_async_copy(src, xs_ref.at[pl.ds(d1_ref[t], 1)], sem).start(priority=0)
            pltpu.make_async_copy(src, xs_ref.at[pl.ds(d2_ref[t], 1)], sem).start(priority=1)
            return carry2

        lax.fori_loop(0, burst, tok, 0, unroll=ISSUE_UNROLL)

        @pl.when(c > 0)
        def _():
            wait_burst()

        return carry

    lax.fori_loop(0, tc // burst, do_burst, 0)
    wait_burst()


def _scatter(d1, d2, h2, xs_init, tc, burst):
    n = h2.shape[0]
    smem_blk = pl.BlockSpec((tc,), lambda i: (i,), memory_space=pltpu.SMEM)
    anyspec = pl.BlockSpec(memory_space=pl.ANY)
    return pl.pallas_call(
        functools.partial(_scatter_kernel, burst=burst),
        grid=(n // tc,),
        in_specs=[smem_blk, smem_blk, pl.BlockSpec((tc, D_MODEL), lambda i: (i, 0)), anyspec],
        out_specs=anyspec,
        out_shape=jax.ShapeDtypeStruct(xs_init.shape, xs_init.dtype),
        scratch_shapes=[pltpu.SemaphoreType.DMA(())],
        input_output_aliases={3: 0},
        compiler_params=_cparams(("arbitrary",)),
        name="moe_scatter",
    )(d1, d2, h2, xs_init)


def _expert_kernel(blk_ref, used_ref, xs_ref, wg_ref, wu_ref, wd_ref, ys_ref,
                   wgb_ref, wub_ref, wdb_ref):
    b = pl.program_id(0)

    @pl.when((b == 0) | (blk_ref[b] != blk_ref[jnp.maximum(b - 1, 0)]))
    def _():
        wgb_ref[...] = wg_ref[0].astype(BF16)
        wub_ref[...] = wu_ref[0].astype(BF16)
        wdb_ref[...] = wd_ref[0].astype(BF16)

    @pl.when(b < used_ref[0])
    def _():
        x = xs_ref[...].astype(BF16)
        gate = jnp.dot(x, wgb_ref[...], preferred_element_type=F32)
        up = jnp.dot(x, wub_ref[...], preferred_element_type=F32)
        hid = (gate * jax.nn.sigmoid(gate) * up).astype(BF16)
        ys_ref[...] = jnp.dot(hid, wdb_ref[...], preferred_element_type=F32)

    @pl.when(b >= used_ref[0])
    def _():
        ys_ref[...] = jnp.zeros_like(ys_ref)


def _experts(blk_e, used, xs, wg, wu, wd):
    p_rows = xs.shape[0]
    n_blk = p_rows // MOE_BLOCK
    grid_spec = pltpu.PrefetchScalarGridSpec(
        num_scalar_prefetch=2,
        grid=(n_blk,),
        in_specs=[
            pl.BlockSpec((MOE_BLOCK, D_MODEL), lambda b, blk, used: (b, 0)),
            pl.BlockSpec((1, D_MODEL, EXPERT_HIDDEN), lambda b, blk, used: (blk[b], 0, 0)),
            pl.BlockSpec((1, D_MODEL, EXPERT_HIDDEN), lambda b, blk, used: (blk[b], 0, 0)),
            pl.BlockSpec((1, EXPERT_HIDDEN, D_MODEL), lambda b, blk, used: (blk[b], 0, 0)),
        ],
        out_specs=pl.BlockSpec((MOE_BLOCK, D_MODEL), lambda b, blk, used: (b, 0)),
        scratch_shapes=[pltpu.VMEM((D_MODEL, EXPERT_HIDDEN), BF16),
                        pltpu.VMEM((D_MODEL, EXPERT_HIDDEN), BF16),
                        pltpu.VMEM((EXPERT_HIDDEN, D_MODEL), BF16)],
    )
    return pl.pallas_call(
        _expert_kernel,
        grid_spec=grid_spec,
        out_shape=jax.ShapeDtypeStruct((p_rows, D_MODEL), F32),
        compiler_params=_cparams(("arbitrary",)),
        name="moe_experts",
    )(blk_e, used, xs, wg, wu, wd)


def _combine_kernel(d1_ref, d2_ref, ys_ref, gate_ref, x1_ref, p_ref, gple_ref, wpg_ref,
                    wple_ref, gfin_ref, o_ref, buf_ref, sem, *, tm, final):
    i = pl.program_id(0)
    n_steps = pl.num_programs(0)

    def issue_row(step, slot, j):
        t = step * tm + j
        pltpu.make_async_copy(ys_ref.at[pl.ds(d1_ref[t], 1)],
                              buf_ref.at[slot, 0, pl.ds(j, 1)], sem.at[slot]).start()
        pltpu.make_async_copy(ys_ref.at[pl.ds(d2_ref[t], 1)],
                              buf_ref.at[slot, 1, pl.ds(j, 1)], sem.at[slot]).start()

    def wait_slot(slot):
        for half in range(2):
            pltpu.make_async_copy(ys_ref.at[pl.ds(0, tm)], buf_ref.at[slot, half],
                                  sem.at[slot]).wait()

    slot = i % 2

    @pl.when(i == 0)
    def _():
        def tok(j, carry):
            issue_row(0, 0, j)
            return carry

        lax.fori_loop(0, tm, tok, 0, unroll=ISSUE_UNROLL)

    wait_slot(slot)
    nxt = jnp.minimum(i + 1, n_steps - 1)
    for j in range(tm):
        issue_row(nxt, 1 - slot, j)

    gates = gate_ref[...]
    x2 = (x1_ref[...] + gates[:, 0:1] * buf_ref[slot, 0] + gates[:, 1:2] * buf_ref[slot, 1])
    hn = _rms(x2, gple_ref[...]).astype(BF16)
    pg = jax.nn.sigmoid(jnp.dot(hn, wpg_ref[...], preferred_element_type=F32))
    pe = jnp.dot(p_ref[...].astype(BF16), wple_ref[...], preferred_element_type=F32)
    x3 = x2 + pe * pg
    if final:
        x3 = _rms(x3, gfin_ref[...])
    o_ref[...] = x3

    @pl.when(i == n_steps - 1)
    def _():
        wait_slot(1 - slot)


def _combine(d1, d2, ys, gates, x1, p2, g_ple, wpg, wple, g_fin, tm, final):
    n = x1.shape[0]
    row = lambda w: pl.BlockSpec((tm, w), lambda i, a, b: (i, 0))
    full = lambda arr: pl.BlockSpec(arr.shape, lambda i, a, b: (0,) * arr.ndim)
    grid_spec = pltpu.PrefetchScalarGridSpec(
        num_scalar_prefetch=2,
        grid=(n // tm,),
        in_specs=[pl.BlockSpec(memory_space=pl.ANY), row(LANES), row(D_MODEL), row(PLE_DIM),
                  full(g_ple), full(wpg), full(wple), full(g_fin)],
        out_specs=row(D_MODEL),
        scratch_shapes=[pltpu.VMEM((2, 2, tm, D_MODEL), F32), pltpu.SemaphoreType.DMA((2,))],
    )
    return pl.pallas_call(
        functools.partial(_combine_kernel, tm=tm, final=final),
        grid_spec=grid_spec,
        out_shape=jax.ShapeDtypeStruct((n, D_MODEL), F32),
        compiler_params=_cparams(("arbitrary",)),
        name="moe_combine_ple",
    )(d1, d2, ys, gates, x1, p2, g_ple, wpg, wple, g_fin)


def _rope_tables(positions):
    half = ROPE_DIM // 2
    inv_freq = ROPE_THETA ** (-jnp.arange(0, ROPE_DIM, 2, dtype=F32) / ROPE_DIM)
    ang = positions.astype(F32).reshape(-1, 1) * inv_freq
    cos, sin = jnp.cos(ang), jnp.sin(ang)
    n = ang.shape[0]
    ones = jnp.ones((n, HEAD_DIM - ROPE_DIM), F32)
    zeros8 = jnp.zeros((n, half), F32)
    zeros48 = jnp.zeros((n, HEAD_DIM - ROPE_DIM), F32)
    c = jnp.concatenate([cos, cos, ones], axis=1)
    s1 = jnp.concatenate([-sin, zeros8, zeros48], axis=1)
    s2 = jnp.concatenate([zeros8, sin, zeros48], axis=1)
    rep = lambda t: jnp.tile(t, (1, LANES // HEAD_DIM))
    return rep(c), rep(s1), rep(s2)


def _pick(limit, n):
    t = min(limit, n)
    assert n % t == 0, (limit, n)
    return t


def kernel(x, p, positions, g_mix, w_in, lam_q1, lam_k1, lam_q2, lam_k2, g_subln, w_br_a, w_br_b, w_o, g_ffn, w_router_group, b_router_group, w_router_expert, b_router_expert, w_exp_gate, w_exp_up, w_exp_down, g_ple, w_ple, w_ple_gate, g_final):
    b, s, d = x.shape
    depth = w_in.shape[0]
    n = b * s
    assert d == D_MODEL and w_in.shape[2] == IN_WIDTH

    tm_in = _pick(512, n)
    tk = _pick(256, s)
    tq_sb = _pick(2 * tk, s)
    tq_da = _pick(4 * tk, s)
    tm_merge = _pick(512, n)
    tc = _pick(1024, n)
    burst = _pick(128, tc)
    tm_comb = _pick(256, n)
    assert tc % 1024 == 0 or tc == n

    n_assign = n * 2
    p_rows = ((n_assign + MOE_BLOCK - 1) // MOE_BLOCK) * MOE_BLOCK + N_EXPERTS * MOE_BLOCK
    n_blk = p_rows // MOE_BLOCK

    rope_c, rope_s1, rope_s2 = _rope_tables(positions)
    x2 = x.reshape(n, d)
    row2 = lambda v: v.reshape(1, -1)

    for i in range(depth):
        lam_init = 0.8 - 0.6 * math.exp(-0.3 * i)
        sbq, sbk, sbv, daq, dak, dav, ga, gb = _inproj(
            x2, row2(g_mix[i]), w_in[i].astype(BF16), rope_c, rope_s1, rope_s2, tm_in)
        r3 = lambda t: t.reshape(b, s, -1)
        ya = _sb_attention(r3(sbq), r3(sbk), r3(sbv), tq_sb, tk)
        yb = _da_attention(r3(daq), r3(dak), r3(dav), row2(lam_q1[i]), row2(lam_k1[i]),
                           row2(lam_q2[i]), row2(lam_k2[i]), g_subln[i].reshape(-1, 1),
                           tq_da, tk, lam_init)

        w_r = jnp.zeros((d, LANES), F32)
        w_r = w_r.at[:, :N_EXPERTS].set(w_router_expert[i])
        w_r = w_r.at[:, GROUP_LANE0:GROUP_LANE0 + N_GROUPS].set(w_router_group[i])
        wr_hi = w_r.astype(BF16)
        wr_lo = (w_r - wr_hi.astype(F32)).astype(BF16)
        b_r = jnp.zeros((1, LANES), F32)
        b_r = b_r.at[0, :N_EXPERTS].set(b_router_expert[i])
        b_r = b_r.at[0, GROUP_LANE0:GROUP_LANE0 + N_GROUPS].set(b_router_group[i])

        x1, h2, route, gates, counts = _merge(
            ya.reshape(n, -1), yb.reshape(n, -1), ga, gb, x2,
            w_br_a[i].astype(BF16), w_br_b[i].astype(BF16), w_o[i].astype(BF16),
            row2(g_ffn[i]), wr_hi, wr_lo, b_r, tm_merge)

        pst, blk_e, used = _plan(counts.reshape(LANES), n_blk)
        dest = _dest(pst, route[0:2].reshape(2, n // LANES, LANES)).reshape(2, n)
        d1, d2 = dest[0], dest[1]
        xs = _scatter(d1, d2, h2, jnp.zeros((p_rows, d), F32), tc, burst)
        ys = _experts(blk_e, used, xs, w_exp_gate[i], w_exp_up[i], w_exp_down[i])
        x2 = _combine(d1, d2, ys, gates, x1, p[i].reshape(n, -1), row2(g_ple[i]),
                      w_ple_gate[i].astype(BF16), w_ple[i].astype(BF16), row2(g_final),
                      tm_comb, final=(i == depth - 1))

    return x2.reshape(b, s, d)
```

```python
import functools
import math

import numpy as np
import jax
import jax.numpy as jnp
from jax import lax
from jax.experimental import pallas as pl
from jax.experimental.pallas import tpu as pltpu

F32 = jnp.float32
BF16 = jnp.bfloat16
I32 = jnp.int32

D_MODEL = 1024
PLE_DIM = 256
RMS_EPS = 1e-6
CHUNK = 64
HEAD_DIM = 64
LANES = 128
SB_WIDTH = 512
DA_WIDTH = 512
ROPE_THETA = 500000.0
ROPE_DIM = 16
N_GROUPS = 4
EXPERTS_PER_GROUP = 8
N_EXPERTS = 32
EXPERT_HIDDEN = 256
MOE_BLOCK = 256
IN_WIDTH = 5120
OFF_SBQ, OFF_SBK, OFF_SBV, OFF_DAQ, OFF_DAK, OFF_DAV, OFF_GA, OFF_GB = (
    0, 512, 1024, 1536, 2048, 2560, 3072, 4096)
QK_SCALE = HEAD_DIM ** -0.5 * math.log2(math.e)
NEG_BIG = -1e30
UNDERFLOW_LOG2 = -160.0
GROUP_LANE0 = N_EXPERTS
VMEM_LIMIT = 56 * 1024 * 1024
ISSUE_UNROLL = 8


def _cparams(sem):
    return pltpu.CompilerParams(dimension_semantics=sem, vmem_limit_bytes=VMEM_LIMIT)


def _rms(x, g):
    return x * lax.rsqrt(jnp.mean(x * x, axis=-1, keepdims=True) + RMS_EPS) * g


def _inproj_kernel(x_ref, g_ref, w_ref, c_ref, s1_ref, s2_ref,
                   sbq_ref, sbk_ref, sbv_ref, daq_ref, dak_ref, dav_ref, ga_ref, gb_ref):
    h = _rms(x_ref[...], g_ref[...]).astype(BF16)

    def proj(off, width):
        return jnp.dot(h, w_ref[:, off:off + width], preferred_element_type=F32)

    sbq_ref[...] = (proj(OFF_SBQ, SB_WIDTH) * QK_SCALE).astype(BF16)
    sbk_ref[...] = proj(OFF_SBK, SB_WIDTH).astype(BF16)
    sbv_ref[...] = proj(OFF_SBV, SB_WIDTH).astype(BF16)

    c = c_ref[...]
    s1 = s1_ref[...]
    s2 = s2_ref[...]

    def rope_store(t, out_ref, scale):
        for j in range(t.shape[1] // LANES):
            tc = t[:, j * LANES:(j + 1) * LANES]
            r = tc * c + pltpu.roll(tc, LANES - 8, 1) * s1 + pltpu.roll(tc, 8, 1) * s2
            out_ref[:, j * LANES:(j + 1) * LANES] = (r * scale).astype(BF16)

    rope_store(proj(OFF_DAQ, DA_WIDTH), daq_ref, QK_SCALE)
    rope_store(proj(OFF_DAK, DA_WIDTH), dak_ref, 1.0)
    dav_ref[...] = proj(OFF_DAV, DA_WIDTH).astype(BF16)
    ga_ref[...] = jax.nn.sigmoid(proj(OFF_GA, D_MODEL)).astype(BF16)
    gb_ref[...] = jax.nn.sigmoid(proj(OFF_GB, D_MODEL)).astype(BF16)


def _inproj(x2, g, w_bf, rope_c, rope_s1, rope_s2, tm):
    n = x2.shape[0]
    row = lambda w: pl.BlockSpec((tm, w), lambda i: (i, 0))
    full = lambda a: pl.BlockSpec(a.shape, lambda i: (0,) * a.ndim)
    outs = [jax.ShapeDtypeStruct((n, w), BF16)
            for w in (SB_WIDTH,) * 3 + (DA_WIDTH,) * 3 + (D_MODEL,) * 2]
    return pl.pallas_call(
        _inproj_kernel,
        grid=(n // tm,),
        in_specs=[row(D_MODEL), full(g), full(w_bf), row(LANES), row(LANES), row(LANES)],
        out_specs=[row(s.shape[1]) for s in outs],
        out_shape=outs,
        compiler_params=_cparams(("parallel",)),
        name="inproj",
    )(x2, g, w_bf, rope_c, rope_s1, rope_s2)


def _kv_block(k_ref, vt_ref, kj, tk):
    start = pl.multiple_of(kj * tk, tk)
    return k_ref[0, pl.ds(start, tk), :], vt_ref[0, kj]


def _split_lane_halves(qt_ref, tk):
    row = lax.broadcasted_iota(I32, (LANES, 1), 0)
    first = row < HEAD_DIM
    out = []
    for s in range(qt_ref.shape[2] // tk):
        qt = qt_ref[0, :, s * tk:(s + 1) * tk]
        zero = jnp.zeros_like(qt)
        out.append((jnp.where(first, qt, zero), jnp.where(first, zero, qt)))
    return out, first


def _staged(items, scores, state, mid, fin, prefetch=None):
    mids = []
    for idx, (c, vt, diag) in enumerate(items):
        payload, state[c] = mid(c, scores[idx](), vt, diag, state[c])
        if prefetch is not None:
            prefetch(c)
        mids.append(payload)
    for (c, vt, diag), payload in zip(items, mids):
        state[c] = fin(c, vt, payload, diag, state[c])
    return state


def _sweep_key_blocks(qi, nsub, tk, k_ref, vt_ref, z_ref, qk, mid, fin, state, alive=None):
    state = list(state)
    nc = 2 * nsub
    items, scores = [], []
    for s_d in reversed(range(nsub)):
        k, vt = _kv_block(k_ref, vt_ref, qi * nsub + s_d, tk)
        for s in range(s_d, nsub):
            for h in range(2):
                c = 2 * s + h
                items.append((c, vt, s == s_d))
                z = qk(c, k)
                scores.append(lambda z=z: z)
    first_kj = jnp.maximum(qi * nsub - 1, 0)
    k_first, _ = _kv_block(k_ref, vt_ref, first_kj, tk)

    seen = set()

    def prefetch_first(c):
        if c not in seen:
            seen.add(c)
            z_ref[c] = qk(c, k_first)

    state = _staged(items, scores, state, mid, fin, prefetch_first)

    def body(i, st):
        kj = qi * nsub - 1 - i
        _, vt = _kv_block(k_ref, vt_ref, kj, tk)
        k_next, _ = _kv_block(k_ref, vt_ref, jnp.maximum(kj - 1, 0), tk)

        def prefetch(c):
            z_ref[c] = qk(c, k_next)

        loop_items = [(c, vt, False) for c in range(nc)]
        loop_scores = [lambda c=c: z_ref[c] for c in range(nc)]
        return tuple(_staged(loop_items, loop_scores, list(st), mid, fin, prefetch))

    n_loop = qi * nsub
    if alive is None:
        return lax.fori_loop(0, n_loop, body, tuple(state))

    def cond(carry):
        i, go, _ = carry
        return (i < n_loop) & go

    def step(carry):
        i, _, st = carry
        st = body(i, st)
        return i + 1, alive(st), st

    return lax.while_loop(cond, step, (jnp.int32(0), alive(state), tuple(state)))[2]


def _sb_kernel(qt_ref, k_ref, vt_ref, ut_ref, o_ref, acc_ref, z_ref, *, tk):
    qi = pl.program_id(2)
    nsub = qt_ref.shape[2] // tk
    qts, first = _split_lane_halves(qt_ref, tk)
    ut = ut_ref[...]
    acc_ref[...] = jnp.zeros_like(acc_ref)
    key = lax.broadcasted_iota(I32, (tk, tk), 0)
    qry = lax.broadcasted_iota(I32, (tk, tk), 1)

    def qk(c, k):
        return jnp.dot(k, qts[c // 2][c % 2], preferred_element_type=F32)

    def mid(c, z, vt, diag, carry):
        sp = jnp.log2(1.0 + jnp.exp2(-jnp.abs(z)))
        ls = jnp.minimum(z, 0.0) - sp
        lstay = ls - z
        if diag:
            lstay = jnp.where(key < qry, lstay, 0.0)
        later = jnp.dot(ut, lstay.astype(BF16), preferred_element_type=F32)
        return (ls, later, lstay[0:1, :]), carry

    def fin(c, vt, payload, diag, carry):
        ls, later, stay0 = payload
        w = jnp.exp2(ls + later + carry)
        if diag:
            w = jnp.where(key < qry, w, 0.0)
        acc_ref[c] += jnp.dot(vt, w.astype(BF16), preferred_element_type=F32)
        return carry + later[0:1, :] + stay0

    def alive(carries):
        top = functools.reduce(jnp.maximum, carries)
        return jnp.max(top) >= UNDERFLOW_LOG2

    zero = jnp.zeros((1, tk), F32)
    _sweep_key_blocks(qi, nsub, tk, k_ref, vt_ref, z_ref, qk, mid, fin, [zero] * (2 * nsub),
                      alive)
    for s in range(nsub):
        o_t = jnp.where(first, acc_ref[2 * s], acc_ref[2 * s + 1])
        o_ref[0, s * tk:(s + 1) * tk, :] = o_t.T.astype(o_ref.dtype)


def _attn_specs(s, tq, tk):
    qt_spec = pl.BlockSpec((1, LANES, tq), lambda bi, p, qi: (bi, p, qi))
    k_spec = pl.BlockSpec((1, s, LANES), lambda bi, p, qi: (bi, 0, p))
    vt_spec = pl.BlockSpec((1, s // tk, LANES, tk), lambda bi, p, qi: (bi, 0, p, 0))
    o_spec = pl.BlockSpec((1, tq, LANES), lambda bi, p, qi: (bi, qi, p))
    return qt_spec, k_spec, vt_spec, o_spec


def _transpose_q(q):
    return jnp.swapaxes(q, 1, 2)


def _transpose_v(v, tk):
    b, s, w = v.shape
    return jnp.swapaxes(v.reshape(b, s // tk, tk, w), 2, 3)


def _sb_attention(q, k, v, tq, tk):
    b, s, width = q.shape
    ut = jnp.asarray(np.triu(np.ones((tk, tk), np.float32), 1), BF16)
    qt_spec, k_spec, vt_spec, o_spec = _attn_specs(s, tq, tk)
    return pl.pallas_call(
        functools.partial(_sb_kernel, tk=tk),
        grid=(b, width // LANES, s // tq),
        in_specs=[qt_spec, k_spec, vt_spec, pl.BlockSpec((tk, tk), lambda bi, p, qi: (0, 0))],
        out_specs=o_spec,
        out_shape=jax.ShapeDtypeStruct((b, s, width), BF16),
        scratch_shapes=[pltpu.VMEM((2 * (tq // tk), LANES, tk), F32),
                        pltpu.VMEM((2 * (tq // tk), tk, tk), F32)],
        compiler_params=_cparams(("parallel", "parallel", "arbitrary")),
        name="sb_attn",
    )(_transpose_q(q), k, _transpose_v(v, tk), ut)


def _da_kernel(qt_ref, k_ref, vt_ref, lq1_ref, lk1_ref, lq2_ref, lk2_ref, g_ref, o_ref,
               acc_ref, z_ref, *, tk, lam_init):
    qi = pl.program_id(2)
    nsub = qt_ref.shape[2] // tk
    qts, _ = _split_lane_halves(qt_ref, tk)
    acc_ref[...] = jnp.zeros_like(acc_ref)
    key = lax.broadcasted_iota(I32, (tk, tk), 0)
    qry = lax.broadcasted_iota(I32, (tk, tk), 1)

    def qk(c, k):
        return jnp.dot(k, qts[c // 2][c % 2], preferred_element_type=F32)

    def mid(c, sc, vt, diag, st):
        mx, l = st
        if diag:
            sc = jnp.where((key // CHUNK) <= (qry // CHUNK), sc, NEG_BIG)
        m_new = jnp.maximum(mx, jnp.max(sc, axis=0, keepdims=True))
        p = jnp.exp2(sc - m_new)
        alpha = jnp.exp2(mx - m_new)
        l = alpha * l + jnp.sum(p, axis=0, keepdims=True)
        pv = jnp.dot(vt, p.astype(BF16), preferred_element_type=F32)
        return (alpha, pv), (m_new, l)

    def fin(c, vt, payload, diag, st):
        alpha, pv = payload
        acc_ref[c] = alpha * acc_ref[c] + pv
        return st

    init = (jnp.full((1, tk), NEG_BIG, F32), jnp.zeros((1, tk), F32))
    state = _sweep_key_blocks(qi, nsub, tk, k_ref, vt_ref, z_ref, qk, mid, fin,
                              [init] * (2 * nsub))

    lam = (jnp.exp(jnp.sum(lq1_ref[...] * lk1_ref[...], axis=-1, keepdims=True))
           - jnp.exp(jnp.sum(lq2_ref[...] * lk2_ref[...], axis=-1, keepdims=True))
           + lam_init)
    for s in range(nsub):
        o_t = (acc_ref[2 * s] / state[2 * s][1]
               - lam * (acc_ref[2 * s + 1] / state[2 * s + 1][1]))
        ms = jnp.mean(o_t * o_t, axis=0, keepdims=True)
        y_t = o_t * lax.rsqrt(ms + RMS_EPS) * g_ref[...] * (1.0 - lam_init)
        o_ref[0, s * tk:(s + 1) * tk, :] = y_t.T.astype(o_ref.dtype)


def _da_attention(q, k, v, lq1, lk1, lq2, lk2, g_col, tq, tk, lam_init):
    b, s, width = q.shape
    vec = lambda a: pl.BlockSpec(a.shape, lambda bi, h, qi: (0, 0))
    qt_spec, k_spec, vt_spec, o_spec = _attn_specs(s, tq, tk)
    return pl.pallas_call(
        functools.partial(_da_kernel, tk=tk, lam_init=lam_init),
        grid=(b, width // LANES, s // tq),
        in_specs=[qt_spec, k_spec, vt_spec, vec(lq1), vec(lk1), vec(lq2), vec(lk2), vec(g_col)],
        out_specs=o_spec,
        out_shape=jax.ShapeDtypeStruct((b, s, width), BF16),
        scratch_shapes=[pltpu.VMEM((2 * (tq // tk), LANES, tk), F32),
                        pltpu.VMEM((2 * (tq // tk), tk, tk), F32)],
        compiler_params=_cparams(("parallel", "parallel", "arbitrary")),
        name="da_attn",
    )(_transpose_q(q), k, _transpose_v(v, tk), lq1, lk1, lq2, lk2, g_col)


def _merge_kernel(ya_ref, yb_ref, ga_ref, gb_ref, x_ref, wa_ref, wb_ref, wo_ref, g_ref,
                  wrh_ref, wrl_ref, br_ref, tri_ref,
                  x1_ref, h2_ref, route_ref, gate_ref, cnt_ref, run_ref):
    @pl.when(pl.program_id(0) == 0)
    def _():
        run_ref[...] = jnp.zeros_like(run_ref)

    pa = jnp.dot(ya_ref[...], wa_ref[...], preferred_element_type=F32)
    pb = jnp.dot(yb_ref[...], wb_ref[...], preferred_element_type=F32)
    merged = ga_ref[...].astype(F32) * pa + gb_ref[...].astype(F32) * pb
    x1 = x_ref[...] + jnp.dot(merged.astype(BF16), wo_ref[...], preferred_element_type=F32)
    x1_ref[...] = x1
    h2 = _rms(x1, g_ref[...])
    h2_ref[...] = h2

    hh = h2.astype(BF16)
    hl = (h2 - hh.astype(F32)).astype(BF16)
    logits = (jnp.dot(hh, wrh_ref[...], preferred_element_type=F32)
              + jnp.dot(hh, wrl_ref[...], preferred_element_type=F32)
              + jnp.dot(hl, wrh_ref[...], preferred_element_type=F32)
              + br_ref[...])

    tm = logits.shape[0]
    lane = lax.broadcasted_iota(I32, (tm, LANES), 1)
    far = jnp.int32(4 * LANES)
    is_group = (lane >= GROUP_LANE0) & (lane < GROUP_LANE0 + N_GROUPS)
    gl = jnp.where(is_group, logits, -jnp.inf)
    gmax = jnp.max(gl, axis=-1, keepdims=True)
    gsel = jnp.min(jnp.where(gl == gmax, lane, far), axis=-1, keepdims=True) - GROUP_LANE0
    gsum = jnp.sum(jnp.where(is_group, jnp.exp(gl - gmax), 0.0), axis=-1, keepdims=True)
    g_p = 1.0 / gsum

    in_group = (lane >> 3) == gsel
    el = jnp.where(in_group, logits, -jnp.inf)
    m1 = jnp.max(el, axis=-1, keepdims=True)
    i1 = jnp.min(jnp.where(el == m1, lane, far), axis=-1, keepdims=True)
    el2 = jnp.where(lane == i1, -jnp.inf, el)
    m2 = jnp.max(el2, axis=-1, keepdims=True)
    i2 = jnp.min(jnp.where(el2 == m2, lane, far), axis=-1, keepdims=True)
    e2 = jnp.exp(m2 - m1)
    gate1 = g_p / (1.0 + e2)
    gate2 = gate1 * e2

    sel1 = lane == i1
    sel2 = lane == i2
    onehot = jnp.where(sel1 | sel2, 1.0, 0.0)
    before = jnp.dot(tri_ref[...], onehot.astype(BF16), preferred_element_type=F32)
    before = before + run_ref[...]
    r1 = jnp.sum(jnp.where(sel1, before, 0.0), axis=-1, keepdims=True).astype(I32)
    r2 = jnp.sum(jnp.where(sel2, before, 0.0), axis=-1, keepdims=True).astype(I32)
    run_ref[...] += jnp.sum(onehot, axis=0, keepdims=True)
    cnt_ref[...] = run_ref[...].astype(I32)

    code1 = (i1 << 16) | r1
    code2 = (i2 << 16) | r2
    route = jnp.where(lane == 0, code1, jnp.where(lane == 1, code2, 0))
    route_ref[...] = route.T[0:8, :]
    gate_ref[...] = jnp.where(lane == 0, gate1, jnp.where(lane == 1, gate2, 0.0))


def _merge(ya, yb, ga, gb, x2, wa, wb, wo, g_ffn, wr_hi, wr_lo, b_r, tm):
    n = x2.shape[0]
    tri = jnp.asarray(np.tril(np.ones((tm, tm), np.float32), -1), BF16)
    row = lambda w: pl.BlockSpec((tm, w), lambda i: (i, 0))
    full = lambda a: pl.BlockSpec(a.shape, lambda i: (0,) * a.ndim)
    return pl.pallas_call(
        _merge_kernel,
        grid=(n // tm,),
        in_specs=[row(SB_WIDTH), row(DA_WIDTH), row(D_MODEL), row(D_MODEL), row(D_MODEL),
                  full(wa), full(wb), full(wo), full(g_ffn), full(wr_hi), full(wr_lo),
                  full(b_r), full(tri)],
        out_specs=[row(D_MODEL), row(D_MODEL), pl.BlockSpec((8, tm), lambda i: (0, i)),
                   row(LANES), pl.BlockSpec((1, LANES), lambda i: (0, 0))],
        out_shape=[jax.ShapeDtypeStruct((n, D_MODEL), F32),
                   jax.ShapeDtypeStruct((n, D_MODEL), F32),
                   jax.ShapeDtypeStruct((8, n), I32),
                   jax.ShapeDtypeStruct((n, LANES), F32),
                   jax.ShapeDtypeStruct((1, LANES), I32)],
        scratch_shapes=[pltpu.VMEM((1, LANES), F32)],
        compiler_params=_cparams(("arbitrary",)),
        name="merge_route",
    )(ya, yb, ga, gb, x2, wa, wb, wo, g_ffn, wr_hi, wr_lo, b_r, tri)


def _plan_kernel(cnt_ref, pst_ref, blk_ref, used_ref, pend_ref, *, n_blk):
    def seg(e, acc):
        pst_ref[e] = acc
        padded = ((cnt_ref[e] + (MOE_BLOCK - 1)) // MOE_BLOCK) * MOE_BLOCK
        pend_ref[e] = acc + padded
        return acc + padded

    total = lax.fori_loop(0, N_EXPERTS, seg, jnp.int32(0))
    used_ref[0] = total // MOE_BLOCK

    def blk(b, e):
        e = lax.while_loop(
            lambda ee: (ee < N_EXPERTS - 1) & (pend_ref[ee] <= b * MOE_BLOCK),
            lambda ee: ee + 1, e)
        blk_ref[b] = e
        return e

    lax.fori_loop(0, n_blk, blk, jnp.int32(0))


def _plan(counts, n_blk):
    smem = pl.BlockSpec(memory_space=pltpu.SMEM)
    return pl.pallas_call(
        functools.partial(_plan_kernel, n_blk=n_blk),
        in_specs=[smem],
        out_specs=[smem, smem, smem],
        out_shape=[jax.ShapeDtypeStruct((N_EXPERTS,), I32),
                   jax.ShapeDtypeStruct((n_blk,), I32),
                   jax.ShapeDtypeStruct((1,), I32)],
        scratch_shapes=[pltpu.SMEM((N_EXPERTS,), I32)],
        name="moe_plan",
    )(counts)


def _dest_kernel(pst_ref, code_ref, d_ref):
    code = code_ref[...]
    expert = code >> 16
    start = jnp.zeros_like(code)
    for e in range(N_EXPERTS):
        start = jnp.where(expert == e, pst_ref[e], start)
    d_ref[...] = start + (code & 0xFFFF)


def _dest(pst, codes):
    return pl.pallas_call(
        _dest_kernel,
        in_specs=[pl.BlockSpec(memory_space=pltpu.SMEM),
                  pl.BlockSpec(memory_space=pltpu.VMEM)],
        out_specs=pl.BlockSpec(memory_space=pltpu.VMEM),
        out_shape=jax.ShapeDtypeStruct(codes.shape, I32),
        name="moe_dest",
    )(pst, codes)


def _scatter_kernel(d1_ref, d2_ref, h_ref, xs_in_ref, xs_ref, sem, *, burst):
    del xs_in_ref
    tc = h_ref.shape[0]

    def wait_burst():
        pltpu.make_async_copy(h_ref.at[pl.ds(0, 2 * burst)], xs_ref.at[pl.ds(0, 2 * burst)],
                              sem).wait()

    def do_burst(c, carry):
        def tok(j, carry2):
            t = c * burst + j
            src = h_ref.at[pl.ds(t, 1)]
            pltpu.make_async_copy(src, xs_ref.at[pl.ds(d1_ref[t], 1)], sem).start(priority=0)
            pltpu.make_async_copy(src, xs_ref.at[pl.ds(d2_ref[t], 1)], sem).start(priority=1)
            return carry2

        lax.fori_loop(0, burst, tok, 0, unroll=ISSUE_UNROLL)

        @pl.when(c > 0)
        def _():
            wait_burst()

        return carry

    lax.fori_loop(0, tc // burst, do_burst, 0)
    wait_burst()


def _scatter(d1, d2, h2, xs_init, tc, burst):
    n = h2.shape[0]
    smem_blk = pl.BlockSpec((tc,), lambda i: (i,), memory_space=pltpu.SMEM)
    anyspec = pl.BlockSpec(memory_space=pl.ANY)
    return pl.pallas_call(
        functools.partial(_scatter_kernel, burst=burst),
        grid=(n // tc,),
        in_specs=[smem_blk, smem_blk, pl.BlockSpec((tc, D_MODEL), lambda i: (i, 0)), anyspec],
        out_specs=anyspec,
        out_shape=jax.ShapeDtypeStruct(xs_init.shape, xs_init.dtype),
        scratch_shapes=[pltpu.SemaphoreType.DMA(())],
        input_output_aliases={3: 0},
        compiler_params=_cparams(("arbitrary",)),
        name="moe_scatter",
    )(d1, d2, h2, xs_init)


def _expert_kernel(blk_ref, used_ref, xs_ref, wg_ref, wu_ref, wd_ref, ys_ref,
                   wgb_ref, wub_ref, wdb_ref):
    b = pl.program_id(0)

    @pl.when((b == 0) | (blk_ref[b] != blk_ref[jnp.maximum(b - 1, 0)]))
    def _():
        wgb_ref[...] = wg_ref[0, 0].astype(BF16)
        wub_ref[...] = wu_ref[0, 0].astype(BF16)
        wdb_ref[...] = wd_ref[0, 0].astype(BF16)

    @pl.when(b < used_ref[0])
    def _():
        x = xs_ref[...].astype(BF16)
        gate = jnp.dot(x, wgb_ref[...], preferred_element_type=F32)
        up = jnp.dot(x, wub_ref[...], preferred_element_type=F32)
        hid = (gate * jax.nn.sigmoid(gate) * up).astype(BF16)
        ys_ref[...] = jnp.dot(hid, wdb_ref[...], preferred_element_type=F32)

    @pl.when(b >= used_ref[0])
    def _():
        ys_ref[...] = jnp.zeros_like(ys_ref)


def _experts(blk_e, used, xs, wg, wu, wd, layer):
    p_rows = xs.shape[0]
    n_blk = p_rows // MOE_BLOCK
    up_spec = pl.BlockSpec((1, 1, D_MODEL, EXPERT_HIDDEN),
                           lambda b, blk, used: (layer, blk[b], 0, 0))
    down_spec = pl.BlockSpec((1, 1, EXPERT_HIDDEN, D_MODEL),
                             lambda b, blk, used: (layer, blk[b], 0, 0))
    grid_spec = pltpu.PrefetchScalarGridSpec(
        num_scalar_prefetch=2,
        grid=(n_blk,),
        in_specs=[
            pl.BlockSpec((MOE_BLOCK, D_MODEL), lambda b, blk, used: (b, 0)),
            up_spec, up_spec, down_spec,
        ],
        out_specs=pl.BlockSpec((MOE_BLOCK, D_MODEL), lambda b, blk, used: (b, 0)),
        scratch_shapes=[pltpu.VMEM((D_MODEL, EXPERT_HIDDEN), BF16),
                        pltpu.VMEM((D_MODEL, EXPERT_HIDDEN), BF16),
                        pltpu.VMEM((EXPERT_HIDDEN, D_MODEL), BF16)],
    )
    return pl.pallas_call(
        _expert_kernel,
        grid_spec=grid_spec,
        out_shape=jax.ShapeDtypeStruct((p_rows, D_MODEL), F32),
        compiler_params=_cparams(("arbitrary",)),
        name="moe_experts",
    )(blk_e, used, xs, wg, wu, wd)


def _combine_kernel(d1_ref, d2_ref, ys_ref, gate_ref, x1_ref, p_ref, gple_ref, wpg_ref,
                    wple_ref, gfin_ref, o_ref, buf_ref, sem, *, tm, final):
    i = pl.program_id(0)
    n_steps = pl.num_programs(0)

    def issue_row(step, slot, j):
        t = step * tm + j
        pltpu.make_async_copy(ys_ref.at[pl.ds(d1_ref[t], 1)],
                              buf_ref.at[slot, 0, pl.ds(j, 1)], sem.at[slot]).start()
        pltpu.make_async_copy(ys_ref.at[pl.ds(d2_ref[t], 1)],
                              buf_ref.at[slot, 1, pl.ds(j, 1)], sem.at[slot]).start()

    def wait_slot(slot):
        for half in range(2):
            pltpu.make_async_copy(ys_ref.at[pl.ds(0, tm)], buf_ref.at[slot, half],
                                  sem.at[slot]).wait()

    slot = i % 2

    @pl.when(i == 0)
    def _():
        def tok(j, carry):
            issue_row(0, 0, j)
            return carry

        lax.fori_loop(0, tm, tok, 0, unroll=ISSUE_UNROLL)

    wait_slot(slot)
    nxt = jnp.minimum(i + 1, n_steps - 1)
    for j in range(tm):
        issue_row(nxt, 1 - slot, j)

    gates = gate_ref[...]
    x2 = (x1_ref[...] + gates[:, 0:1] * buf_ref[slot, 0] + gates[:, 1:2] * buf_ref[slot, 1])
    hn = _rms(x2, gple_ref[...]).astype(BF16)
    pg = jax.nn.sigmoid(jnp.dot(hn, wpg_ref[...], preferred_element_type=F32))
    pe = jnp.dot(p_ref[0].astype(BF16), wple_ref[...], preferred_element_type=F32)
    x3 = x2 + pe * pg
    if final:
        x3 = _rms(x3, gfin_ref[...])
    o_ref[...] = x3

    @pl.when(i == n_steps - 1)
    def _():
        wait_slot(1 - slot)


def _combine(d1, d2, ys, gates, x1, p3, layer, g_ple, wpg, wple, g_fin, tm, final):
    n = x1.shape[0]
    row = lambda w: pl.BlockSpec((tm, w), lambda i, a, b: (i, 0))
    full = lambda arr: pl.BlockSpec(arr.shape, lambda i, a, b: (0,) * arr.ndim)
    grid_spec = pltpu.PrefetchScalarGridSpec(
        num_scalar_prefetch=2,
        grid=(n // tm,),
        in_specs=[pl.BlockSpec(memory_space=pl.ANY), row(LANES), row(D_MODEL),
                  pl.BlockSpec((1, tm, PLE_DIM), lambda i, a, b: (layer, i, 0)),
                  full(g_ple), full(wpg), full(wple), full(g_fin)],
        out_specs=row(D_MODEL),
        scratch_shapes=[pltpu.VMEM((2, 2, tm, D_MODEL), F32), pltpu.SemaphoreType.DMA((2,))],
    )
    return pl.pallas_call(
        functools.partial(_combine_kernel, tm=tm, final=final),
        grid_spec=grid_spec,
        out_shape=jax.ShapeDtypeStruct((n, D_MODEL), F32),
        compiler_params=_cparams(("arbitrary",)),
        name="moe_combine_ple",
    )(d1, d2, ys, gates, x1, p3, g_ple, wpg, wple, g_fin)


def _rope_tables(positions):
    half = ROPE_DIM // 2
    inv_freq = ROPE_THETA ** (-jnp.arange(0, ROPE_DIM, 2, dtype=F32) / ROPE_DIM)
    ang = positions.astype(F32).reshape(-1, 1) * inv_freq
    cos, sin = jnp.cos(ang), jnp.sin(ang)
    n = ang.shape[0]
    ones = jnp.ones((n, HEAD_DIM - ROPE_DIM), F32)
    zeros8 = jnp.zeros((n, half), F32)
    zeros48 = jnp.zeros((n, HEAD_DIM - ROPE_DIM), F32)
    c = jnp.concatenate([cos, cos, ones], axis=1)
    s1 = jnp.concatenate([-sin, zeros8, zeros48], axis=1)
    s2 = jnp.concatenate([zeros8, sin, zeros48], axis=1)
    rep = lambda t: jnp.tile(t, (1, LANES // HEAD_DIM))
    return rep(c), rep(s1), rep(s2)


def _pick(limit, n):
    t = min(limit, n)
    assert n % t == 0, (limit, n)
    return t


def kernel(x, p, positions, g_mix, w_in, lam_q1, lam_k1, lam_q2, lam_k2, g_subln, w_br_a, w_br_b, w_o, g_ffn, w_router_group, b_router_group, w_router_expert, b_router_expert, w_exp_gate, w_exp_up, w_exp_down, g_ple, w_ple, w_ple_gate, g_final):
    b, s, d = x.shape
    depth = w_in.shape[0]
    n = b * s
    assert d == D_MODEL and w_in.shape[2] == IN_WIDTH

    tm_in = _pick(512, n)
    tk = _pick(256, s)
    tq_sb = _pick(2 * tk, s)
    tq_da = _pick(4 * tk, s)
    tm_merge = _pick(512, n)
    tc = _pick(1024, n)
    burst = _pick(128, tc)
    tm_comb = _pick(256, n)
    assert tc % 1024 == 0 or tc == n

    n_assign = n * 2
    p_rows = ((n_assign + MOE_BLOCK - 1) // MOE_BLOCK) * MOE_BLOCK + N_EXPERTS * MOE_BLOCK
    n_blk = p_rows // MOE_BLOCK

    rope_c, rope_s1, rope_s2 = _rope_tables(positions)
    x2 = x.reshape(n, d)
    p3 = p.reshape(depth, n, PLE_DIM)
    xs = jnp.zeros((p_rows, d), F32)
    row2 = lambda v: v.reshape(1, -1)

    for i in range(depth):
        lam_init = 0.8 - 0.6 * math.exp(-0.3 * i)
        sbq, sbk, sbv, daq, dak, dav, ga, gb = _inproj(
            x2, row2(g_mix[i]), w_in[i].astype(BF16), rope_c, rope_s1, rope_s2, tm_in)
        r3 = lambda t: t.reshape(b, s, -1)
        ya = _sb_attention(r3(sbq), r3(sbk), r3(sbv), tq_sb, tk)
        yb = _da_attention(r3(daq), r3(dak), r3(dav), row2(lam_q1[i]), row2(lam_k1[i]),
                           row2(lam_q2[i]), row2(lam_k2[i]), g_subln[i].reshape(-1, 1),
                           tq_da, tk, lam_init)

        w_r = jnp.zeros((d, LANES), F32)
        w_r = w_r.at[:, :N_EXPERTS].set(w_router_expert[i])
        w_r = w_r.at[:, GROUP_LANE0:GROUP_LANE0 + N_GROUPS].set(w_router_group[i])
        wr_hi = w_r.astype(BF16)
        wr_lo = (w_r - wr_hi.astype(F32)).astype(BF16)
        b_r = jnp.zeros((1, LANES), F32)
        b_r = b_r.at[0, :N_EXPERTS].set(b_router_expert[i])
        b_r = b_r.at[0, GROUP_LANE0:GROUP_LANE0 + N_GROUPS].set(b_router_group[i])

        x1, h2, route, gates, counts = _merge(
            ya.reshape(n, -1), yb.reshape(n, -1), ga, gb, x2,
            w_br_a[i].astype(BF16), w_br_b[i].astype(BF16), w_o[i].astype(BF16),
            row2(g_ffn[i]), wr_hi, wr_lo, b_r, tm_merge)

        pst, blk_e, used = _plan(counts.reshape(LANES), n_blk)
        dest = _dest(pst, route[0:2].reshape(2, n // LANES, LANES)).reshape(2, n)
        d1, d2 = dest[0], dest[1]
        xs = _scatter(d1, d2, h2, xs, tc, burst)
        ys = _experts(blk_e, used, xs, w_exp_gate, w_exp_up, w_exp_down, i)
        x2 = _combine(d1, d2, ys, gates, x1, p3, i, row2(g_ple[i]),
                      w_ple_gate[i].astype(BF16), w_ple[i].astype(BF16), row2(g_final),
                      tm_comb, final=(i == depth - 1))

    return x2.reshape(b, s, d)
```

```python
import functools
import math

import numpy as np
import jax
import jax.numpy as jnp
from jax import lax
from jax.experimental import pallas as pl
from jax.experimental.pallas import tpu as pltpu

F32 = jnp.float32
BF16 = jnp.bfloat16
I32 = jnp.int32

D_MODEL = 1024
PLE_DIM = 256
RMS_EPS = 1e-6
CHUNK = 64
HEAD_DIM = 64
LANES = 128
SB_WIDTH = 512
DA_WIDTH = 512
ROPE_THETA = 500000.0
ROPE_DIM = 16
N_GROUPS = 4
EXPERTS_PER_GROUP = 8
N_EXPERTS = 32
EXPERT_HIDDEN = 256
MOE_BLOCK = 256
IN_WIDTH = 5120
OFF_SBQ, OFF_SBK, OFF_SBV, OFF_DAQ, OFF_DAK, OFF_DAV, OFF_GA, OFF_GB = (
    0, 512, 1024, 1536, 2048, 2560, 3072, 4096)
QK_SCALE = HEAD_DIM ** -0.5 * math.log2(math.e)
NEG_BIG = -1e30
UNDERFLOW_LOG2 = -160.0
GROUP_LANE0 = N_EXPERTS
VMEM_LIMIT = 56 * 1024 * 1024
ISSUE_UNROLL = 8


def _cparams(sem):
    return pltpu.CompilerParams(dimension_semantics=sem, vmem_limit_bytes=VMEM_LIMIT)


def _rms(x, g):
    return x * lax.rsqrt(jnp.mean(x * x, axis=-1, keepdims=True) + RMS_EPS) * g


def _inproj_kernel(x_ref, g_ref, w_ref, c_ref, s1_ref, s2_ref,
                   sbq_ref, sbk_ref, sbv_ref, daq_ref, dak_ref, dav_ref, ga_ref, gb_ref):
    h = _rms(x_ref[...], g_ref[...]).astype(BF16)

    def proj(off, width):
        return jnp.dot(h, w_ref[:, off:off + width], preferred_element_type=F32)

    sbq_ref[...] = (proj(OFF_SBQ, SB_WIDTH) * QK_SCALE).astype(BF16)
    sbk_ref[...] = proj(OFF_SBK, SB_WIDTH).astype(BF16)
    sbv_ref[...] = proj(OFF_SBV, SB_WIDTH).astype(BF16)

    c = c_ref[...]
    s1 = s1_ref[...]
    s2 = s2_ref[...]

    def rope_store(t, out_ref, scale):
        for j in range(t.shape[1] // LANES):
            tc = t[:, j * LANES:(j + 1) * LANES]
            r = tc * c + pltpu.roll(tc, LANES - 8, 1) * s1 + pltpu.roll(tc, 8, 1) * s2
            out_ref[:, j * LANES:(j + 1) * LANES] = (r * scale).astype(BF16)

    rope_store(proj(OFF_DAQ, DA_WIDTH), daq_ref, QK_SCALE)
    rope_store(proj(OFF_DAK, DA_WIDTH), dak_ref, 1.0)
    dav_ref[...] = proj(OFF_DAV, DA_WIDTH).astype(BF16)
    ga_ref[...] = jax.nn.sigmoid(proj(OFF_GA, D_MODEL)).astype(BF16)
    gb_ref[...] = jax.nn.sigmoid(proj(OFF_GB, D_MODEL)).astype(BF16)


def _inproj(x2, g, w_bf, rope_c, rope_s1, rope_s2, tm):
    n = x2.shape[0]
    row = lambda w: pl.BlockSpec((tm, w), lambda i: (i, 0))
    full = lambda a: pl.BlockSpec(a.shape, lambda i: (0,) * a.ndim)
    outs = [jax.ShapeDtypeStruct((n, w), BF16)
            for w in (SB_WIDTH,) * 3 + (DA_WIDTH,) * 3 + (D_MODEL,) * 2]
    return pl.pallas_call(
        _inproj_kernel,
        grid=(n // tm,),
        in_specs=[row(D_MODEL), full(g), full(w_bf), row(LANES), row(LANES), row(LANES)],
        out_specs=[row(s.shape[1]) for s in outs],
        out_shape=outs,
        compiler_params=_cparams(("parallel",)),
        name="inproj",
    )(x2, g, w_bf, rope_c, rope_s1, rope_s2)


def _kv_block(k_ref, vt_ref, kj, tk):
    start = pl.multiple_of(kj * tk, tk)
    return k_ref[0, pl.ds(start, tk), :], vt_ref[0, kj]


def _split_lane_halves(qt_ref, tk):
    row = lax.broadcasted_iota(I32, (LANES, 1), 0)
    first = row < HEAD_DIM
    out = []
    for s in range(qt_ref.shape[2] // tk):
        qt = qt_ref[0, :, s * tk:(s + 1) * tk]
        zero = jnp.zeros_like(qt)
        out.append((jnp.where(first, qt, zero), jnp.where(first, zero, qt)))
    return out, first


def _staged(items, scores, state, mid, fin, prefetch=None):
    mids = []
    for idx, (c, vt, diag) in enumerate(items):
        payload, state[c] = mid(c, scores[idx](), vt, diag, state[c])
        if prefetch is not None:
            prefetch(c)
        mids.append(payload)
    for (c, vt, diag), payload in zip(items, mids):
        state[c] = fin(c, vt, payload, diag, state[c])
    return state


def _sweep_key_blocks(qi, nsub, tk, k_ref, vt_ref, z_ref, qk, mid, fin, state, alive=None):
    state = list(state)
    nc = 2 * nsub
    items, scores = [], []
    for s_d in reversed(range(nsub)):
        k, vt = _kv_block(k_ref, vt_ref, qi * nsub + s_d, tk)
        for s in range(s_d, nsub):
            for h in range(2):
                c = 2 * s + h
                items.append((c, vt, s == s_d))
                z = qk(c, k)
                scores.append(lambda z=z: z)
    first_kj = jnp.maximum(qi * nsub - 1, 0)
    k_first, _ = _kv_block(k_ref, vt_ref, first_kj, tk)

    seen = set()

    def prefetch_first(c):
        if c not in seen:
            seen.add(c)
            z_ref[c] = qk(c, k_first)

    state = _staged(items, scores, state, mid, fin, prefetch_first)

    def body(i, st):
        kj = qi * nsub - 1 - i
        _, vt = _kv_block(k_ref, vt_ref, kj, tk)
        k_next, _ = _kv_block(k_ref, vt_ref, jnp.maximum(kj - 1, 0), tk)

        def prefetch(c):
            z_ref[c] = qk(c, k_next)

        loop_items = [(c, vt, False) for c in range(nc)]
        loop_scores = [lambda c=c: z_ref[c] for c in range(nc)]
        return tuple(_staged(loop_items, loop_scores, list(st), mid, fin, prefetch))

    n_loop = qi * nsub
    if alive is None:
        return lax.fori_loop(0, n_loop, body, tuple(state))

    def cond(carry):
        i, go, _ = carry
        return (i < n_loop) & go

    def step(carry):
        i, _, st = carry
        st = body(i, st)
        return i + 1, alive(st), st

    return lax.while_loop(cond, step, (jnp.int32(0), alive(state), tuple(state)))[2]


def _sb_kernel(qt_ref, k_ref, vt_ref, ut_ref, o_ref, acc_ref, z_ref, *, tk):
    qi = pl.program_id(2)
    nsub = qt_ref.shape[2] // tk
    qts, first = _split_lane_halves(qt_ref, tk)
    ut = ut_ref[...]
    acc_ref[...] = jnp.zeros_like(acc_ref)
    key = lax.broadcasted_iota(I32, (tk, tk), 0)
    qry = lax.broadcasted_iota(I32, (tk, tk), 1)

    def qk(c, k):
        return jnp.dot(k, qts[c // 2][c % 2], preferred_element_type=F32)

    def mid(c, z, vt, diag, carry):
        sp = jnp.log2(1.0 + jnp.exp2(-jnp.abs(z)))
        ls = jnp.minimum(z, 0.0) - sp
        lstay = ls - z
        if diag:
            lstay = jnp.where(key < qry, lstay, 0.0)
        later = jnp.dot(ut, lstay.astype(BF16), preferred_element_type=F32)
        return (ls, later, lstay[0:1, :]), carry

    def fin(c, vt, payload, diag, carry):
        ls, later, stay0 = payload
        w = jnp.exp2(ls + later + carry)
        if diag:
            w = jnp.where(key < qry, w, 0.0)
        acc_ref[c] += jnp.dot(vt, w.astype(BF16), preferred_element_type=F32)
        return carry + later[0:1, :] + stay0

    def alive(carries):
        top = functools.reduce(jnp.maximum, carries)
        return jnp.max(top) >= UNDERFLOW_LOG2

    zero = jnp.zeros((1, tk), F32)
    _sweep_key_blocks(qi, nsub, tk, k_ref, vt_ref, z_ref, qk, mid, fin, [zero] * (2 * nsub),
                      alive)
    for s in range(nsub):
        o_t = jnp.where(first, acc_ref[2 * s], acc_ref[2 * s + 1])
        o_ref[0, s * tk:(s + 1) * tk, :] = o_t.T.astype(o_ref.dtype)


def _attn_specs(s, tq, tk):
    qt_spec = pl.BlockSpec((1, LANES, tq), lambda bi, p, qi: (bi, p, qi))
    k_spec = pl.BlockSpec((1, s, LANES), lambda bi, p, qi: (bi, 0, p))
    vt_spec = pl.BlockSpec((1, s // tk, LANES, tk), lambda bi, p, qi: (bi, 0, p, 0))
    o_spec = pl.BlockSpec((1, tq, LANES), lambda bi, p, qi: (bi, qi, p))
    return qt_spec, k_spec, vt_spec, o_spec


def _transpose_q(q):
    return jnp.swapaxes(q, 1, 2)


def _transpose_v(v, tk):
    b, s, w = v.shape
    return jnp.swapaxes(v.reshape(b, s // tk, tk, w), 2, 3)


def _sb_attention(q, k, v, tq, tk):
    b, s, width = q.shape
    ut = jnp.asarray(np.triu(np.ones((tk, tk), np.float32), 1), BF16)
    qt_spec, k_spec, vt_spec, o_spec = _attn_specs(s, tq, tk)
    return pl.pallas_call(
        functools.partial(_sb_kernel, tk=tk),
        grid=(b, width // LANES, s // tq),
        in_specs=[qt_spec, k_spec, vt_spec, pl.BlockSpec((tk, tk), lambda bi, p, qi: (0, 0))],
        out_specs=o_spec,
        out_shape=jax.ShapeDtypeStruct((b, s, width), BF16),
        scratch_shapes=[pltpu.VMEM((2 * (tq // tk), LANES, tk), F32),
                        pltpu.VMEM((2 * (tq // tk), tk, tk), F32)],
        compiler_params=_cparams(("parallel", "parallel", "arbitrary")),
        name="sb_attn",
    )(_transpose_q(q), k, _transpose_v(v, tk), ut)


def _da_kernel(qt_ref, k_ref, vt_ref, lq1_ref, lk1_ref, lq2_ref, lk2_ref, g_ref, o_ref,
               acc_ref, z_ref, *, tk, lam_init):
    qi = pl.program_id(2)
    nsub = qt_ref.shape[2] // tk
    qts, _ = _split_lane_halves(qt_ref, tk)
    acc_ref[...] = jnp.zeros_like(acc_ref)
    key = lax.broadcasted_iota(I32, (tk, tk), 0)
    qry = lax.broadcasted_iota(I32, (tk, tk), 1)

    def qk(c, k):
        return jnp.dot(k, qts[c // 2][c % 2], preferred_element_type=F32)

    def mid(c, sc, vt, diag, st):
        mx, l = st
        if diag:
            sc = jnp.where((key // CHUNK) <= (qry // CHUNK), sc, NEG_BIG)
        m_new = jnp.maximum(mx, jnp.max(sc, axis=0, keepdims=True))
        p = jnp.exp2(sc - m_new)
        alpha = jnp.exp2(mx - m_new)
        l = alpha * l + jnp.sum(p, axis=0, keepdims=True)
        pv = jnp.dot(vt, p.astype(BF16), preferred_element_type=F32)
        return (alpha, pv), (m_new, l)

    def fin(c, vt, payload, diag, st):
        alpha, pv = payload
        acc_ref[c] = alpha * acc_ref[c] + pv
        return st

    init = (jnp.full((1, tk), NEG_BIG, F32), jnp.zeros((1, tk), F32))
    state = _sweep_key_blocks(qi, nsub, tk, k_ref, vt_ref, z_ref, qk, mid, fin,
                              [init] * (2 * nsub))

    lam = (jnp.exp(jnp.sum(lq1_ref[...] * lk1_ref[...], axis=-1, keepdims=True))
           - jnp.exp(jnp.sum(lq2_ref[...] * lk2_ref[...], axis=-1, keepdims=True))
           + lam_init)
    for s in range(nsub):
        o_t = (acc_ref[2 * s] / state[2 * s][1]
               - lam * (acc_ref[2 * s + 1] / state[2 * s + 1][1]))
        ms = jnp.mean(o_t * o_t, axis=0, keepdims=True)
        y_t = o_t * lax.rsqrt(ms + RMS_EPS) * g_ref[...] * (1.0 - lam_init)
        o_ref[0, s * tk:(s + 1) * tk, :] = y_t.T.astype(o_ref.dtype)


def _da_attention(q, k, v, lq1, lk1, lq2, lk2, g_col, tq, tk, lam_init):
    b, s, width = q.shape
    vec = lambda a: pl.BlockSpec(a.shape, lambda bi, h, qi: (0, 0))
    qt_spec, k_spec, vt_spec, o_spec = _attn_specs(s, tq, tk)
    return pl.pallas_call(
        functools.partial(_da_kernel, tk=tk, lam_init=lam_init),
        grid=(b, width // LANES, s // tq),
        in_specs=[qt_spec, k_spec, vt_spec, vec(lq1), vec(lk1), vec(lq2), vec(lk2), vec(g_col)],
        out_specs=o_spec,
        out_shape=jax.ShapeDtypeStruct((b, s, width), BF16),
        scratch_shapes=[pltpu.VMEM((2 * (tq // tk), LANES, tk), F32),
                        pltpu.VMEM((2 * (tq // tk), tk, tk), F32)],
        compiler_params=_cparams(("parallel", "parallel", "arbitrary")),
        name="da_attn",
    )(_transpose_q(q), k, _transpose_v(v, tk), lq1, lk1, lq2, lk2, g_col)


def _merge_kernel(ya_ref, yb_ref, ga_ref, gb_ref, x_ref, wa_ref, wb_ref, wo_ref, g_ref,
                  wrh_ref, wrl_ref, br_ref, tri_ref,
                  x1_ref, h2_ref, route_ref, gate_ref, cnt_ref, run_ref):
    @pl.when(pl.program_id(0) == 0)
    def _():
        run_ref[...] = jnp.zeros_like(run_ref)

    pa = jnp.dot(ya_ref[...], wa_ref[...], preferred_element_type=F32)
    pb = jnp.dot(yb_ref[...], wb_ref[...], preferred_element_type=F32)
    merged = ga_ref[...].astype(F32) * pa + gb_ref[...].astype(F32) * pb
    x1 = x_ref[...] + jnp.dot(merged.astype(BF16), wo_ref[...], preferred_element_type=F32)
    x1_ref[...] = x1
    h2 = _rms(x1, g_ref[...])
    h2_ref[...] = h2

    hh = h2.astype(BF16)
    hl = (h2 - hh.astype(F32)).astype(BF16)
    logits = (jnp.dot(hh, wrh_ref[...], preferred_element_type=F32)
              + jnp.dot(hh, wrl_ref[...], preferred_element_type=F32)
              + jnp.dot(hl, wrh_ref[...], preferred_element_type=F32)
              + br_ref[...])

    tm = logits.shape[0]
    lane = lax.broadcasted_iota(I32, (tm, LANES), 1)
    far = jnp.int32(4 * LANES)
    is_group = (lane >= GROUP_LANE0) & (lane < GROUP_LANE0 + N_GROUPS)
    gl = jnp.where(is_group, logits, -jnp.inf)
    gmax = jnp.max(gl, axis=-1, keepdims=True)
    gsel = jnp.min(jnp.where(gl == gmax, lane, far), axis=-1, keepdims=True) - GROUP_LANE0
    gsum = jnp.sum(jnp.where(is_group, jnp.exp(gl - gmax), 0.0), axis=-1, keepdims=True)
    g_p = 1.0 / gsum

    in_group = (lane >> 3) == gsel
    el = jnp.where(in_group, logits, -jnp.inf)
    m1 = jnp.max(el, axis=-1, keepdims=True)
    i1 = jnp.min(jnp.where(el == m1, lane, far), axis=-1, keepdims=True)
    el2 = jnp.where(lane == i1, -jnp.inf, el)
    m2 = jnp.max(el2, axis=-1, keepdims=True)
    i2 = jnp.min(jnp.where(el2 == m2, lane, far), axis=-1, keepdims=True)
    e2 = jnp.exp(m2 - m1)
    gate1 = g_p / (1.0 + e2)
    gate2 = gate1 * e2

    sel1 = lane == i1
    sel2 = lane == i2
    onehot = jnp.where(sel1 | sel2, 1.0, 0.0)
    before = jnp.dot(tri_ref[...], onehot.astype(BF16), preferred_element_type=F32)
    before = before + run_ref[...]
    r1 = jnp.sum(jnp.where(sel1, before, 0.0), axis=-1, keepdims=True).astype(I32)
    r2 = jnp.sum(jnp.where(sel2, before, 0.0), axis=-1, keepdims=True).astype(I32)
    run_ref[...] += jnp.sum(onehot, axis=0, keepdims=True)
    cnt_ref[...] = run_ref[...].astype(I32)

    code1 = (i1 << 16) | r1
    code2 = (i2 << 16) | r2
    route = jnp.where(lane == 0, code1, jnp.where(lane == 1, code2, 0))
    route_ref[...] = route.T[0:8, :]
    gate_ref[...] = jnp.where(lane == 0, gate1, jnp.where(lane == 1, gate2, 0.0))


def _merge(ya, yb, ga, gb, x2, wa, wb, wo, g_ffn, wr_hi, wr_lo, b_r, tm):
    n = x2.shape[0]
    tri = jnp.asarray(np.tril(np.ones((tm, tm), np.float32), -1), BF16)
    row = lambda w: pl.BlockSpec((tm, w), lambda i: (i, 0))
    full = lambda a: pl.BlockSpec(a.shape, lambda i: (0,) * a.ndim)
    return pl.pallas_call(
        _merge_kernel,
        grid=(n // tm,),
        in_specs=[row(SB_WIDTH), row(DA_WIDTH), row(D_MODEL), row(D_MODEL), row(D_MODEL),
                  full(wa), full(wb), full(wo), full(g_ffn), full(wr_hi), full(wr_lo),
                  full(b_r), full(tri)],
        out_specs=[row(D_MODEL), row(D_MODEL), pl.BlockSpec((8, tm), lambda i: (0, i)),
                   row(LANES), pl.BlockSpec((1, LANES), lambda i: (0, 0))],
        out_shape=[jax.ShapeDtypeStruct((n, D_MODEL), F32),
                   jax.ShapeDtypeStruct((n, D_MODEL), F32),
                   jax.ShapeDtypeStruct((8, n), I32),
                   jax.ShapeDtypeStruct((n, LANES), F32),
                   jax.ShapeDtypeStruct((1, LANES), I32)],
        scratch_shapes=[pltpu.VMEM((1, LANES), F32)],
        compiler_params=_cparams(("arbitrary",)),
        name="merge_route",
    )(ya, yb, ga, gb, x2, wa, wb, wo, g_ffn, wr_hi, wr_lo, b_r, tri)


def _plan_kernel(cnt_ref, pst_ref, blk_ref, used_ref, pend_ref, *, n_blk):
    def seg(e, acc):
        pst_ref[e] = acc
        padded = ((cnt_ref[e] + (MOE_BLOCK - 1)) // MOE_BLOCK) * MOE_BLOCK
        pend_ref[e] = acc + padded
        return acc + padded

    total = lax.fori_loop(0, N_EXPERTS, seg, jnp.int32(0))
    used_ref[0] = total // MOE_BLOCK

    def blk(b, e):
        e = lax.while_loop(
            lambda ee: (ee < N_EXPERTS - 1) & (pend_ref[ee] <= b * MOE_BLOCK),
            lambda ee: ee + 1, e)
        blk_ref[b] = e
        return e

    lax.fori_loop(0, n_blk, blk, jnp.int32(0))


def _plan(counts, n_blk):
    smem = pl.BlockSpec(memory_space=pltpu.SMEM)
    return pl.pallas_call(
        functools.partial(_plan_kernel, n_blk=n_blk),
        in_specs=[smem],
        out_specs=[smem, smem, smem],
        out_shape=[jax.ShapeDtypeStruct((N_EXPERTS,), I32),
                   jax.ShapeDtypeStruct((n_blk,), I32),
                   jax.ShapeDtypeStruct((1,), I32)],
        scratch_shapes=[pltpu.SMEM((N_EXPERTS,), I32)],
        name="moe_plan",
    )(counts)


def _dest_kernel(pst_ref, code_ref, d_ref):
    code = code_ref[...]
    expert = code >> 16
    start = jnp.zeros_like(code)
    for e in range(N_EXPERTS):
        start = jnp.where(expert == e, pst_ref[e], start)
    d_ref[...] = start + (code & 0xFFFF)


def _dest(pst, codes):
    return pl.pallas_call(
        _dest_kernel,
        in_specs=[pl.BlockSpec(memory_space=pltpu.SMEM),
                  pl.BlockSpec(memory_space=pltpu.VMEM)],
        out_specs=pl.BlockSpec(memory_space=pltpu.VMEM),
        out_shape=jax.ShapeDtypeStruct(codes.shape, I32),
        name="moe_dest",
    )(pst, codes)


def _scatter_kernel(d1_ref, d2_ref, h_ref, xs_in_ref, xs_ref, sem, *, burst):
    del xs_in_ref
    tc = h_ref.shape[0]

    def wait_burst():
        pltpu.make_async_copy(h_ref.at[pl.ds(0, 2 * burst)], xs_ref.at[pl.ds(0, 2 * burst)],
                              sem).wait()

    def do_burst(c, carry):
        def tok(j, carry2):
            t = c * burst + j
            src = h_ref.at[pl.ds(t, 1)]
            pltpu.make_async_copy(src, xs_ref.at[pl.ds(d1_ref[t], 1)], sem).start(priority=0)
            pltpu.make_async_copy(src, xs_ref.at[pl.ds(d2_ref[t], 1)], sem).start(priority=1)
            return carry2

        lax.fori_loop(0, burst, tok, 0, unroll=ISSUE_UNROLL)

        @pl.when(c > 0)
        def _():
            wait_burst()

        return carry

    lax.fori_loop(0, tc // burst, do_burst, 0)
    wait_burst()


def _scatter(d1, d2, h2, xs_init, tc, burst):
    n = h2.shape[0]
    smem_blk = pl.BlockSpec((tc,), lambda i: (i,), memory_space=pltpu.SMEM)
    anyspec = pl.BlockSpec(memory_space=pl.ANY)
    return pl.pallas_call(
        functools.partial(_scatter_kernel, burst=burst),
        grid=(n // tc,),
        in_specs=[smem_blk, smem_blk, pl.BlockSpec((tc, D_MODEL), lambda i: (i, 0)), anyspec],
        out_specs=anyspec,
        out_shape=jax.ShapeDtypeStruct(xs_init.shape, xs_init.dtype),
        scratch_shapes=[pltpu.SemaphoreType.DMA(())],
        input_output_aliases={3: 0},
        compiler_params=_cparams(("arbitrary",)),
        name="moe_scatter",
    )(d1, d2, h2, xs_init)


def _expert_kernel(blk_ref, used_ref, xs_ref, wg_ref, wu_ref, wd_ref, ys_ref,
                   wgb_ref, wub_ref, wdb_ref):
    b = pl.program_id(0)

    @pl.when((b == 0) | (blk_ref[b] != blk_ref[jnp.maximum(b - 1, 0)]))
    def _():
        wgb_ref[...] = wg_ref[0, 0].astype(BF16)
        wub_ref[...] = wu_ref[0, 0].astype(BF16)
        wdb_ref[...] = wd_ref[0, 0].astype(BF16)

    @pl.when(b < used_ref[0])
    def _():
        x = xs_ref[...].astype(BF16)
        gate = jnp.dot(x, wgb_ref[...], preferred_element_type=F32)
        up = jnp.dot(x, wub_ref[...], preferred_element_type=F32)
        hid = (gate * jax.nn.sigmoid(gate) * up).astype(BF16)
        ys_ref[...] = jnp.dot(hid, wdb_ref[...], preferred_element_type=F32)

    @pl.when(b >= used_ref[0])
    def _():
        ys_ref[...] = jnp.zeros_like(ys_ref)


def _experts(blk_e, used, xs, wg, wu, wd, layer):
    p_rows = xs.shape[0]
    n_blk = p_rows // MOE_BLOCK
    up_spec = pl.BlockSpec((1, 1, D_MODEL, EXPERT_HIDDEN),
                           lambda b, blk, used: (layer, blk[b], 0, 0))
    down_spec = pl.BlockSpec((1, 1, EXPERT_HIDDEN, D_MODEL),
                             lambda b, blk, used: (layer, blk[b], 0, 0))
    grid_spec = pltpu.PrefetchScalarGridSpec(
        num_scalar_prefetch=2,
        grid=(n_blk,),
        in_specs=[
            pl.BlockSpec((MOE_BLOCK, D_MODEL), lambda b, blk, used: (b, 0)),
            up_spec, up_spec, down_spec,
        ],
        out_specs=pl.BlockSpec((MOE_BLOCK, D_MODEL), lambda b, blk, used: (b, 0)),
        scratch_shapes=[pltpu.VMEM((D_MODEL, EXPERT_HIDDEN), BF16),
                        pltpu.VMEM((D_MODEL, EXPERT_HIDDEN), BF16),
                        pltpu.VMEM((EXPERT_HIDDEN, D_MODEL), BF16)],
    )
    return pl.pallas_call(
        _expert_kernel,
        grid_spec=grid_spec,
        out_shape=jax.ShapeDtypeStruct((p_rows, D_MODEL), F32),
        compiler_params=_cparams(("arbitrary",)),
        name="moe_experts",
    )(blk_e, used, xs, wg, wu, wd)


def _combine_kernel(d1_ref, d2_ref, ys_ref, gate_ref, x1_ref, p_ref, gple_ref, wpg_ref,
                    wple_ref, gfin_ref, o_ref, buf_ref, sem, *, tm, final):
    i = pl.program_id(0)
    n_steps = pl.num_programs(0)

    def issue_row(step, slot, j):
        t = step * tm + j
        pltpu.make_async_copy(ys_ref.at[pl.ds(d1_ref[t], 1)],
                              buf_ref.at[slot, 0, pl.ds(j, 1)], sem.at[slot]).start()
        pltpu.make_async_copy(ys_ref.at[pl.ds(d2_ref[t], 1)],
                              buf_ref.at[slot, 1, pl.ds(j, 1)], sem.at[slot]).start()

    def wait_slot(slot):
        for half in range(2):
            pltpu.make_async_copy(ys_ref.at[pl.ds(0, tm)], buf_ref.at[slot, half],
                                  sem.at[slot]).wait()

    slot = i % 2

    @pl.when(i == 0)
    def _():
        def tok(j, carry):
            issue_row(0, 0, j)
            return carry

        lax.fori_loop(0, tm, tok, 0, unroll=ISSUE_UNROLL)

    wait_slot(slot)
    nxt = jnp.minimum(i + 1, n_steps - 1)
    for j in range(tm):
        issue_row(nxt, 1 - slot, j)

    gates = gate_ref[...]
    x2 = (x1_ref[...] + gates[:, 0:1] * buf_ref[slot, 0] + gates[:, 1:2] * buf_ref[slot, 1])
    hn = _rms(x2, gple_ref[...]).astype(BF16)
    pg = jax.nn.sigmoid(jnp.dot(hn, wpg_ref[...], preferred_element_type=F32))
    pe = jnp.dot(p_ref[0].astype(BF16), wple_ref[...], preferred_element_type=F32)
    x3 = x2 + pe * pg
    if final:
        x3 = _rms(x3, gfin_ref[...])
    o_ref[...] = x3

    @pl.when(i == n_steps - 1)
    def _():
        wait_slot(1 - slot)


def _combine(d1, d2, ys, gates, x1, p3, layer, g_ple, wpg, wple, g_fin, tm, final):
    n = x1.shape[0]
    row = lambda w: pl.BlockSpec((tm, w), lambda i, a, b: (i, 0))
    full = lambda arr: pl.BlockSpec(arr.shape, lambda i, a, b: (0,) * arr.ndim)
    grid_spec = pltpu.PrefetchScalarGridSpec(
        num_scalar_prefetch=2,
        grid=(n // tm,),
        in_specs=[pl.BlockSpec(memory_space=pl.ANY), row(LANES), row(D_MODEL),
                  pl.BlockSpec((1, tm, PLE_DIM), lambda i, a, b: (layer, i, 0)),
                  full(g_ple), full(wpg), full(wple), full(g_fin)],
        out_specs=row(D_MODEL),
        scratch_shapes=[pltpu.VMEM((2, 2, tm, D_MODEL), F32), pltpu.SemaphoreType.DMA((2,))],
    )
    return pl.pallas_call(
        functools.partial(_combine_kernel, tm=tm, final=final),
        grid_spec=grid_spec,
        out_shape=jax.ShapeDtypeStruct((n, D_MODEL), F32),
        compiler_params=_cparams(("arbitrary",)),
        name="moe_combine_ple",
    )(d1, d2, ys, gates, x1, p3, g_ple, wpg, wple, g_fin)


def _rope_tables(positions):
    half = ROPE_DIM // 2
    inv_freq = ROPE_THETA ** (-jnp.arange(0, ROPE_DIM, 2, dtype=F32) / ROPE_DIM)
    ang = positions.astype(F32).reshape(-1, 1) * inv_freq
    cos, sin = jnp.cos(ang), jnp.sin(ang)
    n = ang.shape[0]
    ones = jnp.ones((n, HEAD_DIM - ROPE_DIM), F32)
    zeros8 = jnp.zeros((n, half), F32)
    zeros48 = jnp.zeros((n, HEAD_DIM - ROPE_DIM), F32)
    c = jnp.concatenate([cos, cos, ones], axis=1)
    s1 = jnp.concatenate([-sin, zeros8, zeros48], axis=1)
    s2 = jnp.concatenate([zeros8, sin, zeros48], axis=1)
    rep = lambda t: jnp.tile(t, (1, LANES // HEAD_DIM))
    return rep(c), rep(s1), rep(s2)


def _pick(limit, n):
    t = min(limit, n)
    assert n % t == 0, (limit, n)
    return t


def kernel(x, p, positions, g_mix, w_in, lam_q1, lam_k1, lam_q2, lam_k2, g_subln, w_br_a, w_br_b, w_o, g_ffn, w_router_group, b_router_group, w_router_expert, b_router_expert, w_exp_gate, w_exp_up, w_exp_down, g_ple, w_ple, w_ple_gate, g_final):
    b, s, d = x.shape
    depth = w_in.shape[0]
    n = b * s
    assert d == D_MODEL and w_in.shape[2] == IN_WIDTH

    tm_in = _pick(512, n)
    tk = _pick(256, s)
    tq_sb = _pick(2 * tk, s)
    tq_da = _pick(8 * tk, s)
    tm_merge = _pick(512, n)
    tc = _pick(1024, n)
    burst = _pick(128, tc)
    tm_comb = _pick(256, n)
    assert tc % 1024 == 0 or tc == n

    n_assign = n * 2
    p_rows = ((n_assign + MOE_BLOCK - 1) // MOE_BLOCK) * MOE_BLOCK + N_EXPERTS * MOE_BLOCK
    n_blk = p_rows // MOE_BLOCK

    rope_c, rope_s1, rope_s2 = _rope_tables(positions)
    x2 = x.reshape(n, d)
    p3 = p.reshape(depth, n, PLE_DIM)
    xs = jnp.zeros((p_rows, d), F32)
    row2 = lambda v: v.reshape(1, -1)

    for i in range(depth):
        lam_init = 0.8 - 0.6 * math.exp(-0.3 * i)
        sbq, sbk, sbv, daq, dak, dav, ga, gb = _inproj(
            x2, row2(g_mix[i]), w_in[i].astype(BF16), rope_c, rope_s1, rope_s2, tm_in)
        r3 = lambda t: t.reshape(b, s, -1)
        ya = _sb_attention(r3(sbq), r3(sbk), r3(sbv), tq_sb, tk)
        yb = _da_attention(r3(daq), r3(dak), r3(dav), row2(lam_q1[i]), row2(lam_k1[i]),
                           row2(lam_q2[i]), row2(lam_k2[i]), g_subln[i].reshape(-1, 1),
                           tq_da, 2 * tk, lam_init)

        w_r = jnp.zeros((d, LANES), F32)
        w_r = w_r.at[:, :N_EXPERTS].set(w_router_expert[i])
        w_r = w_r.at[:, GROUP_LANE0:GROUP_LANE0 + N_GROUPS].set(w_router_group[i])
        wr_hi = w_r.astype(BF16)
        wr_lo = (w_r - wr_hi.astype(F32)).astype(BF16)
        b_r = jnp.zeros((1, LANES), F32)
        b_r = b_r.at[0, :N_EXPERTS].set(b_router_expert[i])
        b_r = b_r.at[0, GROUP_LANE0:GROUP_LANE0 + N_GROUPS].set(b_router_group[i])

        x1, h2, route, gates, counts = _merge(
            ya.reshape(n, -1), yb.reshape(n, -1), ga, gb, x2,
            w_br_a[i].astype(BF16), w_br_b[i].astype(BF16), w_o[i].astype(BF16),
            row2(g_ffn[i]), wr_hi, wr_lo, b_r, tm_merge)

        pst, blk_e, used = _plan(counts.reshape(LANES), n_blk)
        dest = _dest(pst, route[0:2].reshape(2, n // LANES, LANES)).reshape(2, n)
        d1, d2 = dest[0], dest[1]
        xs = _scatter(d1, d2, h2, xs, tc, burst)
        ys = _experts(blk_e, used, xs, w_exp_gate, w_exp_up, w_exp_down, i)
        x2 = _combine(d1, d2, ys, gates, x1, p3, i, row2(g_ple[i]),
                      w_ple_gate[i].astype(BF16), w_ple[i].astype(BF16), row2(g_final),
                      tm_comb, final=(i == depth - 1))

    return x2.reshape(b, s, d)
```
